```python
import jax, jax.numpy as jnp
from jax import lax
import numpy as np


D_MODEL = 1024
BATCH = 16
SEQ = 2048
DEPTH = 2
DEC_BATCH = 8
DEC_SEQ = 64
PAST_LEN = 4096

CHUNK = 64
POOL_WIDTH = 512
POOL_GROUPS = 4
POOL_GROUP_WIDTH = POOL_WIDTH // POOL_GROUPS
POOL_WINDOWS = (2, 4, 8, 16)
POOL_STATE = max(POOL_WINDOWS) - 1
N_HEADS = 8
HEAD_DIM = 64
ATTN_WIDTH = N_HEADS * HEAD_DIM
IDX_HEADS = 8
IDX_DIM = 32
TOPK_MAX = 256
QUERY_BLOCK = 128
ROPE_THETA = 10000.0
EPS = 1e-6
IN_SIZES = (POOL_WIDTH, POOL_WIDTH, ATTN_WIDTH, ATTN_WIDTH, ATTN_WIDTH,
            IDX_HEADS * IDX_DIM, IDX_DIM, IDX_HEADS, ATTN_WIDTH, D_MODEL, D_MODEL)
IN_COLS = sum(IN_SIZES)

kernel_name = 'streaming_pool_dsa_hybrid_step'


def rms_norm(x, g):
    xf = x.astype(jnp.float32)
    r = lax.rsqrt(jnp.mean(xf * xf, axis=-1, keepdims=True) + EPS)
    return (xf * r * g.astype(jnp.float32)).astype(x.dtype)


def rope(x, pos):
    d = x.shape[-1]
    inv = ROPE_THETA ** (-jnp.arange(0, d, 2, dtype=jnp.float32) / d)
    ang = pos.astype(jnp.float32)[:, None] * inv[None, :]
    cos = jnp.cos(ang)[:, None, :]
    sin = jnp.sin(ang)[:, None, :]
    xf = x.astype(jnp.float32)
    x1, x2 = xf[..., : d // 2], xf[..., d // 2:]
    return jnp.concatenate([x1 * cos - x2 * sin, x2 * cos + x1 * sin], axis=-1).astype(x.dtype)


def pool_mix(u, left, pos, w_mix, scale):
    B, T, P = u.shape
    up = jnp.concatenate([left, u], axis=1).astype(jnp.float32)
    c = jnp.concatenate([jnp.zeros((B, 1, P), jnp.float32), jnp.cumsum(up, axis=1)], axis=1)
    end = c[:, POOL_STATE + 1:]
    means = []
    for gi, w in enumerate(POOL_WINDOWS):
        sl = slice(gi * POOL_GROUP_WIDTH, (gi + 1) * POOL_GROUP_WIDTH)
        start = c[:, POOL_STATE + 1 - w: POOL_STATE + 1 - w + T, sl]
        cnt = jnp.minimum(pos + 1, w).astype(jnp.float32)[None, :, None]
        means.append((end[..., sl] - start) / cnt)
    pooled = jnp.concatenate(means, axis=-1) - up[:, POOL_STATE:]
    mixed = jnp.einsum('btgc,gcd->btgd',
                       pooled.reshape(B, T, POOL_GROUPS, POOL_GROUP_WIDTH),
                       w_mix.astype(jnp.float32)).reshape(B, T, P)
    mixed = mixed * scale.astype(jnp.float32)
    new_state = up[:, -POOL_STATE:].astype(u.dtype)
    return mixed.astype(u.dtype), new_state


def dsa_attend(q, qi, wi, k, v, ki, q_pos, k_pos, k_sel):
    B, T, H, Dh = q.shape
    qb = min(QUERY_BLOCK, T)
    nb = T // qb

    def blk(r):
        return r.reshape((B * nb, qb) + r.shape[2:])

    xs = (blk(q), blk(qi), blk(wi),
          jnp.tile(q_pos.reshape(nb, qb), (B, 1)),
          jnp.repeat(jnp.arange(B, dtype=jnp.int32), nb))
    k_chunk = k_pos // CHUNK

    def one(args):
        qB, qiB, wB, pB, b = args
        kB, vB, kiB = k[b], v[b], ki[b]
        s = jnp.einsum('thd,sd->ths', qiB.astype(jnp.float32), kiB.astype(jnp.float32)) * IDX_DIM ** -0.5
        score = jnp.einsum('th,ths->ts', wB.astype(jnp.float32), jax.nn.relu(s)) * IDX_HEADS ** -0.5
        adm = k_chunk[None, :] <= (pB // CHUNK)[:, None]
        score = jnp.where(adm, score, -jnp.inf)
        _, sel = lax.top_k(score, k_sel)
        ok = jnp.take_along_axis(adm, sel, axis=1)
        kg = kB[sel].astype(jnp.float32)
        vg = vB[sel].astype(jnp.float32)
        logit = jnp.einsum('thd,tkhd->thk', qB.astype(jnp.float32), kg) * Dh ** -0.5
        logit = jnp.where(ok[:, None, :], logit, -jnp.inf)
        p = jax.nn.softmax(logit, axis=-1)
        return jnp.einsum('thk,tkhd->thd', p, vg).astype(q.dtype)

    out = lax.map(one, xs)
    return out.reshape(B, T, H * Dh)


def layer(x, pos, k_pos, past_k, past_v, past_ki, pool_left, k_sel,
          norm_g, w_in, w_pool_mix, pool_scale, w_pool_out, w_attn_out, w_o):
    B, T, _ = x.shape
    h = rms_norm(x, norm_g)
    proj = h @ w_in
    split_points = np.cumsum(IN_SIZES)[:-1].tolist()
    u_p, z_p, q, k, v, qi, ki, wi, z_a, g_p, g_a = jnp.split(proj, split_points, axis=-1)
    pooled, pool_state = pool_mix(u_p, pool_left, pos, w_pool_mix, pool_scale)
    br_pool = (pooled * jax.nn.silu(z_p)) @ w_pool_out
    q = rope(q.reshape(B, T, N_HEADS, HEAD_DIM), pos)
    k = rope(k.reshape(B, T, N_HEADS, HEAD_DIM), pos)
    v = v.reshape(B, T, N_HEADS, HEAD_DIM)
    qi = rope(qi.reshape(B, T, IDX_HEADS, IDX_DIM), pos)
    ki = rope(ki[:, :, None, :], pos)[:, :, 0]
    if past_k is None:
        k_all, v_all, ki_all = k, v, ki
    else:
        k_all = jnp.concatenate([past_k, k], axis=1)
        v_all = jnp.concatenate([past_v, v], axis=1)
        ki_all = jnp.concatenate([past_ki, ki], axis=1)
    o = dsa_attend(q, qi, wi, k_all, v_all, ki_all, pos, k_pos, k_sel)
    br_attn = (o * jax.nn.silu(z_a)) @ w_attn_out
    merged = jax.nn.sigmoid(g_p) * br_pool + jax.nn.sigmoid(g_a) * br_attn
    x = x + merged @ w_o
    return x, k, v, ki, pool_state


def setup_inputs(seed: int = 0) -> dict:
    key = jax.random.key(seed)
    ks = jax.random.split(key, 16)
    f32 = jnp.float32
    return {
        'x_prompt': jax.random.normal(ks[0], (BATCH, SEQ, D_MODEL), f32),
        'x_sample': jax.random.normal(ks[1], (DEC_BATCH, DEC_SEQ, D_MODEL), f32),
        'cache_k': jax.random.normal(ks[2], (DEPTH, DEC_BATCH, PAST_LEN, N_HEADS, HEAD_DIM), f32),
        'cache_v': jax.random.normal(ks[3], (DEPTH, DEC_BATCH, PAST_LEN, N_HEADS, HEAD_DIM), f32),
        'cache_kidx': jax.random.normal(ks[4], (DEPTH, DEC_BATCH, PAST_LEN, IDX_DIM), f32),
        'state_pool': jax.random.normal(ks[5], (DEPTH, DEC_BATCH, POOL_STATE, POOL_WIDTH), f32),
        'norm_g': 1.0 + 0.02 * jax.random.normal(ks[6], (DEPTH, D_MODEL), f32),
        'w_in': jax.random.normal(ks[7], (DEPTH, D_MODEL, IN_COLS), f32) * D_MODEL ** -0.5,
        'w_pool_mix': jax.random.normal(ks[8], (DEPTH, POOL_GROUPS, POOL_GROUP_WIDTH, POOL_GROUP_WIDTH), f32) * POOL_GROUP_WIDTH ** -0.5,
        'pool_scale': 1.0 + 0.02 * jax.random.normal(ks[9], (DEPTH, POOL_WIDTH), f32),
        'w_pool_out': jax.random.normal(ks[10], (DEPTH, POOL_WIDTH, D_MODEL), f32) * POOL_WIDTH ** -0.5,
        'w_attn_out': jax.random.normal(ks[11], (DEPTH, ATTN_WIDTH, D_MODEL), f32) * ATTN_WIDTH ** -0.5,
        'w_o': jax.random.normal(ks[12], (DEPTH, D_MODEL, D_MODEL), f32) * D_MODEL ** -0.5,
        'final_norm_g': 1.0 + 0.02 * jax.random.normal(ks[13], (D_MODEL,), f32),
    }


def reference(x_prompt, x_sample, cache_k, cache_v, cache_kidx, state_pool,
              norm_g, w_in, w_pool_mix, pool_scale, w_pool_out, w_attn_out, w_o, final_norm_g):
    Bp, Tp, _ = x_prompt.shape
    Bs, Ts, _ = x_sample.shape
    past = cache_k.shape[2]
    pos_p = jnp.arange(Tp, dtype=jnp.int32)
    pos_s = past + jnp.arange(Ts, dtype=jnp.int32)
    kpos_s = jnp.arange(past + Ts, dtype=jnp.int32)
    ksel_p = min(TOPK_MAX, Tp // 4)
    ksel_s = min(TOPK_MAX, (past + Ts) // 4)
    zero_left = jnp.zeros((Bp, POOL_STATE, POOL_WIDTH), x_prompt.dtype)

    xp, xs = x_prompt, x_sample
    kp, vp, kip, pp = [], [], [], []
    ksl, vsl, kisl, psl = [], [], [], []
    for l in range(DEPTH):
        wts = (norm_g[l], w_in[l], w_pool_mix[l], pool_scale[l], w_pool_out[l], w_attn_out[l], w_o[l])
        xp, k1, v1, ki1, p1 = layer(xp, pos_p, pos_p, None, None, None, zero_left, ksel_p, *wts)
        xs, k2, v2, ki2, p2 = layer(xs, pos_s, kpos_s, cache_k[l], cache_v[l], cache_kidx[l],
                                    state_pool[l], ksel_s, *wts)
        kp.append(k1); vp.append(v1); kip.append(ki1); pp.append(p1)
        ksl.append(k2); vsl.append(v2); kisl.append(ki2); psl.append(p2)

    y_prompt = rms_norm(xp, final_norm_g)
    y_sample = rms_norm(xs, final_norm_g)
    return (y_prompt, y_sample,
            jnp.stack(kp), jnp.stack(vp), jnp.stack(kip), jnp.stack(pp),
            jnp.stack(ksl), jnp.stack(vsl), jnp.stack(kisl), jnp.stack(psl))
```

```python
import functools

import jax
import jax.numpy as jnp
import numpy as np
from jax import lax
from jax.experimental import pallas as pl
from jax.experimental.pallas import tpu as pltpu

F32 = jnp.float32
BF16 = jnp.bfloat16
I32 = jnp.int32

D_MODEL = 1024
CHUNK = 64
POOL_WIDTH = 512
POOL_WINDOWS = (2, 4, 8, 16)
POOL_GROUP_WIDTH = 128
POOL_STATE = 15
N_HEADS = 8
HEAD_DIM = 64
ATTN_WIDTH = N_HEADS * HEAD_DIM
IDX_HEADS = 8
IDX_DIM = 32
TOPK_MAX = 256
ROPE_THETA = 10000.0
EPS = 1e-6

LANES = 128
HALO = 16
INT_MIN = -2147483648
NEG_BIG = -1e30
VMEM_LIMIT = 56 * 1024 * 1024

SEG_U, SEG_ZP, SEG_Q, SEG_K, SEG_V = 0, 512, 1024, 1536, 2048
SEG_QI, SEG_ZA, SEG_GP, SEG_GA, SEG_MISC = 2560, 2816, 3328, 4352, 5376
PROJ_COLS = 5504
_ORIG = np.cumsum([0, 512, 512, 512, 512, 512, 256, 32, 8, 512, 1024, 1024])


def _prep_w_in(w):
    o = _ORIG
    main = [w[:, o[0]:o[6]], w[:, o[8]:o[11]]]
    misc = jnp.concatenate([w[:, o[6]:o[8]], jnp.zeros((w.shape[0], LANES - IDX_DIM - IDX_HEADS), w.dtype)], axis=1)
    return jnp.concatenate(main + [misc], axis=1).astype(BF16)


def _rope_tables(pos):
    lane = jnp.arange(LANES)

    def tabs(d, passthrough_from=None):
        half = d // 2
        inv = ROPE_THETA ** (-jnp.arange(0, d, 2, dtype=F32) / d)
        ang = pos.astype(F32)[:, None] * inv[None, :]
        cos = jnp.cos(ang)[:, lane % half]
        sin = jnp.sin(ang)[:, lane % half]
        lo = (lane % d) < half
        sin_lo = jnp.where(lo[None, :], -sin, 0.0)
        sin_hi = jnp.where(lo[None, :], 0.0, sin)
        if passthrough_from is not None:
            keep = (lane < passthrough_from)[None, :]
            cos = jnp.where(keep, cos, 1.0)
            sin_lo = jnp.where(keep, sin_lo, 0.0)
            sin_hi = jnp.where(keep, sin_hi, 0.0)
        return [cos, sin_lo, sin_hi]

    return jnp.stack(tabs(HEAD_DIM) + tabs(IDX_DIM) + tabs(IDX_DIM, IDX_DIM)).astype(F32)


def _rope(x, cos, sin_lo, sin_hi, half):
    return x * cos + pltpu.roll(x, LANES - half, 1) * sin_lo + pltpu.roll(x, half, 1) * sin_hi


def _silu(x):
    return x * jax.nn.sigmoid(x)


def _proj_kernel(x_ref, left_ref, tab_ref, g_ref, win_ref, wmix_ref, pscale_ref, wpo_ref,
                 gp_ref, q_ref, k_ref, v_ref, qi_ref, slab_ref, sza_ref, sga_ref, pstate_ref,
                 h_scr, u_scr, pg_scr, *, tm, past):
    t = pl.program_id(1)
    x = x_ref[0]
    r = lax.rsqrt(jnp.mean(x * x, axis=-1, keepdims=True) + EPS)
    h_scr[...] = (x * r * g_ref[...]).astype(BF16)

    def seg(a, n):
        return jnp.dot(h_scr[...], win_ref[:, a:a + n], preferred_element_type=F32)

    @pl.when(t == 0)
    def _():
        u_scr[0:HALO, :] = left_ref[0]

    u_scr[HALO:HALO + tm, :] = seg(SEG_U, POOL_WIDTH)
    pos = past + t * tm + lax.broadcasted_iota(I32, (tm, 1), 0)
    zp = seg(SEG_ZP, POOL_WIDTH)
    for gi, w in enumerate(POOL_WINDOWS):
        cols = slice(gi * POOL_GROUP_WIDTH, (gi + 1) * POOL_GROUP_WIDTH)
        u_g = u_scr[HALO:HALO + tm, cols]
        s = u_g
        for j in range(1, w):
            s = s + u_scr[HALO - j:HALO - j + tm, cols]
        cnt = jnp.minimum(pos + 1, w).astype(F32)
        pooled = s / cnt - u_g
        mixed = jnp.dot(pooled.astype(BF16), wmix_ref[gi], preferred_element_type=F32) * pscale_ref[:, cols]
        pg_scr[:, cols] = (mixed * _silu(zp[:, cols])).astype(BF16)
    br_pool = jnp.dot(pg_scr[...], wpo_ref[...], preferred_element_type=F32)
    gp_ref[0] = jax.nn.sigmoid(seg(SEG_GP, D_MODEL)) * br_pool
    pstate_ref[0] = u_scr[tm:tm + HALO, :]
    u_scr[0:HALO, :] = u_scr[tm:tm + HALO, :]

    c64, lo64, hi64 = tab_ref[0], tab_ref[1], tab_ref[2]
    c32, lo32, hi32 = tab_ref[3], tab_ref[4], tab_ref[5]
    cm, lom, him = tab_ref[6], tab_ref[7], tab_ref[8]
    qs = seg(SEG_Q, ATTN_WIDTH)
    ks = seg(SEG_K, ATTN_WIDTH)
    for c in range(ATTN_WIDTH // LANES):
        cols = slice(c * LANES, (c + 1) * LANES)
        q_ref[0, :, cols] = (_rope(qs[:, cols], c64, lo64, hi64, HEAD_DIM // 2) * HEAD_DIM ** -0.5).astype(BF16)
        k_ref[0, :, cols] = _rope(ks[:, cols], c64, lo64, hi64, HEAD_DIM // 2)
    v_ref[0] = seg(SEG_V, ATTN_WIDTH)
    qis = seg(SEG_QI, IDX_HEADS * IDX_DIM)
    for c in range(IDX_HEADS * IDX_DIM // LANES):
        cols = slice(c * LANES, (c + 1) * LANES)
        qi_ref[0, :, cols] = _rope(qis[:, cols], c32, lo32, hi32, IDX_DIM // 2).astype(BF16)
    slab_ref[0] = _rope(seg(SEG_MISC, LANES), cm, lom, him, IDX_DIM // 2)
    sza_ref[0] = _silu(seg(SEG_ZA, ATTN_WIDTH))
    sga_ref[0] = jax.nn.sigmoid(seg(SEG_GA, D_MODEL))


def _proj(x, left16, tabs, g, w_in, w_mix, pscale, w_po, *, past, tm):
    B, T, _ = x.shape
    nt = T // tm
    const = dict(pipeline_mode=pl.Buffered(1))
    row = lambda n: pl.BlockSpec((1, tm, n), lambda b, t: (b, t, 0))
    out_shape = [
        jax.ShapeDtypeStruct((B, T, D_MODEL), F32),
        jax.ShapeDtypeStruct((B, T, ATTN_WIDTH), BF16),
        jax.ShapeDtypeStruct((B, T, ATTN_WIDTH), F32),
        jax.ShapeDtypeStruct((B, T, ATTN_WIDTH), F32),
        jax.ShapeDtypeStruct((B, T, IDX_HEADS * IDX_DIM), BF16),
        jax.ShapeDtypeStruct((B, T, LANES), F32),
        jax.ShapeDtypeStruct((B, T, ATTN_WIDTH), F32),
        jax.ShapeDtypeStruct((B, T, D_MODEL), F32),
        jax.ShapeDtypeStruct((B, HALO, POOL_WIDTH), F32),
    ]
    return pl.pallas_call(
        functools.partial(_proj_kernel, tm=tm, past=past),
        grid=(B, nt),
        in_specs=[
            row(D_MODEL),
            pl.BlockSpec((1, HALO, POOL_WIDTH), lambda b, t: (b, 0, 0)),
            pl.BlockSpec((9, tm, LANES), lambda b, t: (0, t, 0)),
            pl.BlockSpec((1, D_MODEL), lambda b, t: (0, 0), **const),
            pl.BlockSpec((D_MODEL, PROJ_COLS), lambda b, t: (0, 0), **const),
            pl.BlockSpec((4, POOL_GROUP_WIDTH, POOL_GROUP_WIDTH), lambda b, t: (0, 0, 0), **const),
            pl.BlockSpec((1, POOL_WIDTH), lambda b, t: (0, 0), **const),
            pl.BlockSpec((POOL_WIDTH, D_MODEL), lambda b, t: (0, 0), **const),
        ],
        out_specs=[row(D_MODEL), row(ATTN_WIDTH), row(ATTN_WIDTH), row(ATTN_WIDTH),
                   row(IDX_HEADS * IDX_DIM), row(LANES), row(ATTN_WIDTH), row(D_MODEL),
                   pl.BlockSpec((1, HALO, POOL_WIDTH), lambda b, t: (b, 0, 0))],
        out_shape=out_shape,
        scratch_shapes=[pltpu.VMEM((tm, D_MODEL), BF16),
                        pltpu.VMEM((tm + HALO, POOL_WIDTH), F32),
                        pltpu.VMEM((tm, POOL_WIDTH), BF16)],
        compiler_params=pltpu.CompilerParams(dimension_semantics=("arbitrary", "arbitrary"),
                                             vmem_limit_bytes=VMEM_LIMIT),
        name="proj",
    )(x, left16, tabs, g, w_in, w_mix, pscale, w_po)


def _pad_rows(a, rows):
    if a.shape[0] == rows:
        return a
    return jnp.concatenate([a, jnp.zeros((rows - a.shape[0],) + a.shape[1:], a.dtype)], axis=0)


def _attend_kernel(*refs, qb, ql, kc, n_past, past, total, ksel, has_past):
    if has_past:
        (q_ref, qi_ref, slab_ref, k_ref, v_ref, pk_ref, pv_ref, pki_ref, rep_ref, tri_ref,
         o_ref, kbuf, vtbuf, kibuf, key_scr, bias_scr, ot_scr) = refs
    else:
        (q_ref, qi_ref, slab_ref, k_ref, v_ref, rep_ref, tri_ref,
         o_ref, kbuf, vtbuf, kibuf, key_scr, bias_scr, ot_scr) = refs
    s = pl.program_id(1)
    nt = (((0,), (0,)), ((), ()))
    del nt
    contract_last = (((1,), (1,)), ((), ()))

    if has_past:
        @pl.when(s < n_past)
        def _():
            kbuf[s] = pk_ref[0].astype(BF16)
            vtbuf[s] = pv_ref[0].T.astype(BF16)
            kibuf[s] = jnp.dot(pki_ref[0].astype(BF16), rep_ref[0:IDX_DIM, :],
                               preferred_element_type=F32).astype(BF16)

    @pl.when(s >= n_past)
    def _():
        i = s - n_past
        c = n_past + i
        slab = _pad_rows(slab_ref[0], kc)
        kbuf[c] = _pad_rows(k_ref[0], kc).astype(BF16)
        vtbuf[c] = _pad_rows(v_ref[0], kc).T.astype(BF16)
        kibuf[c] = jnp.dot(slab.astype(BF16), rep_ref[...], preferred_element_type=F32).astype(BF16)

        w_t = _pad_rows(slab_ref[0], ql).T[IDX_DIM:IDX_DIM + IDX_HEADS, :] * (IDX_DIM ** -0.5 * IDX_HEADS ** -0.5)
        qi = _pad_rows(qi_ref[0], ql)
        lane = lax.broadcasted_iota(I32, (ql, LANES), 1)
        heads_per_tile = LANES // IDX_DIM

        def score_keys(j):
            ki = kibuf[j]
            acc = jnp.zeros((kc, ql), F32)
            for h in range(IDX_HEADS):
                tile = h // heads_per_tile
                qm = jnp.where(lane // IDX_DIM == h % heads_per_tile,
                               qi[:, tile * LANES:(tile + 1) * LANES], jnp.zeros((), BF16))
                sh = lax.dot_general(ki, qm, contract_last, preferred_element_type=F32)
                acc = acc + w_t[h:h + 1, :] * jnp.maximum(sh, 0.0)
            bits = pltpu.bitcast(acc, I32)
            key = bits ^ ((bits >> 31) & 0x7FFFFFFF)
            return jnp.where(acc == 0.0, 0, key)

        def full_chunk(j, carry):
            key_scr[j] = score_keys(j)
            return carry

        lax.fori_loop(0, c, full_chunk, 0)
        kpos = c * kc + lax.broadcasted_iota(I32, (kc, ql), 0)
        qpos = past + i * qb + lax.broadcasted_iota(I32, (kc, ql), 1)
        adm = (kpos // CHUNK <= qpos // CHUNK) & (kpos < total)
        key_scr[c] = jnp.where(adm, score_keys(c), INT_MIN)

        def count(cand, strict):
            def body(j, acc):
                kj = key_scr[j]
                hit = (kj > cand) if strict else (kj >= cand)
                return acc + jnp.where(hit, 1.0, 0.0).reshape(kc // 8, 8, ql).sum(axis=0)
            acc = lax.fori_loop(0, c + 1, body, jnp.zeros((8, ql), F32))
            return acc.sum(axis=0, keepdims=True)

        def bit_step(p, prefix):
            cand = prefix ^ (jnp.int32(1) << (31 - p))
            return jnp.where(count(cand, False) >= ksel, cand, prefix)

        thr = lax.fori_loop(0, 32, bit_step, jnp.full((1, ql), INT_MIN, I32))
        need = ksel - count(thr, True)
        live = thr != INT_MIN

        def select_chunk(j, seen):
            kj = key_scr[j]
            eq = kj == thr
            eqf = jnp.where(eq, 1.0, 0.0)
            rank = seen + jnp.dot(tri_ref[...], eqf.astype(BF16), preferred_element_type=F32)
            sel = (kj > thr) | (eq & (rank < need) & live)
            bias_scr[j] = jnp.where(sel, 0.0, NEG_BIG)
            return seen + eqf.reshape(kc // 8, 8, ql).sum(axis=0).sum(axis=0, keepdims=True)

        lax.fori_loop(0, c + 1, select_chunk, jnp.zeros((1, ql), F32))

        q = _pad_rows(q_ref[0], ql)
        for h in range(N_HEADS):
            pair = h // 2
            qm = jnp.where((lane // HEAD_DIM) == h % 2, q[:, pair * LANES:(pair + 1) * LANES], jnp.zeros((), BF16))

            def attend_chunk(j, carry, qm=qm, pair=pair, h=h):
                m, l, acc = carry
                kj = kbuf[j, :, pair * LANES:(pair + 1) * LANES]
                sc = lax.dot_general(kj, qm, contract_last, preferred_element_type=F32) + bias_scr[j]
                m_new = jnp.maximum(m, sc.max(axis=0, keepdims=True))
                p = jnp.exp(sc - m_new)
                alpha = jnp.exp(m - m_new)
                l = alpha * l + p.sum(axis=0, keepdims=True)
                vt = vtbuf[j, h * HEAD_DIM:(h + 1) * HEAD_DIM, :]
                acc = alpha * acc + jnp.dot(vt, p.astype(BF16), preferred_element_type=F32)
                return m_new, l, acc

            init = (jnp.full((1, ql), NEG_BIG, F32), jnp.zeros((1, ql), F32), jnp.zeros((HEAD_DIM, ql), F32))
            _, l, acc = lax.fori_loop(0, c + 1, attend_chunk, init)
            ot_scr[h * HEAD_DIM:(h + 1) * HEAD_DIM, :] = acc / l
        o_ref[0] = ot_scr[...].T[0:qb, :]


def _attend(q, qi, slab, k, v, past_kv, *, qb, kc):
    B, T, _ = q.shape
    has_past = past_kv is not None
    past = past_kv[0].shape[1] if has_past else 0
    total = past + T
    assert past % kc == 0 and T % qb == 0 and (qb == kc or T == qb)
    n_past = past // kc
    nb = T // qb
    nc = n_past + (T + kc - 1) // kc
    ql = max(qb, LANES)
    ksel = min(TOPK_MAX, total // 4)
    rep = (jnp.arange(LANES)[:, None] == (jnp.arange(LANES)[None, :] % IDX_DIM)) & (jnp.arange(LANES)[:, None] < IDX_DIM)
    tri = jnp.arange(kc)[:, None] > jnp.arange(kc)[None, :]

    cur = lambda n: pl.BlockSpec((1, qb, n), lambda b, s: (b, jnp.maximum(s - n_past, 0), 0))
    in_specs = [cur(ATTN_WIDTH), cur(IDX_HEADS * IDX_DIM), cur(LANES), cur(ATTN_WIDTH), cur(ATTN_WIDTH)]
    args = [q, qi, slab, k, v]
    if has_past:
        old = lambda n: pl.BlockSpec((1, kc, n), lambda b, s: (b, jnp.minimum(s, n_past - 1), 0))
        in_specs += [old(ATTN_WIDTH), old(ATTN_WIDTH), old(IDX_DIM)]
        args += list(past_kv)
    in_specs += [pl.BlockSpec((LANES, LANES), lambda b, s: (0, 0)), pl.BlockSpec((kc, kc), lambda b, s: (0, 0))]
    args += [rep.astype(BF16), tri.astype(BF16)]
    return pl.pallas_call(
        functools.partial(_attend_kernel, qb=qb, ql=ql, kc=kc, n_past=n_past, past=past, total=total,
                          ksel=ksel, has_past=has_past),
        grid=(B, n_past + nb),
        in_specs=in_specs,
        out_specs=cur(ATTN_WIDTH),
        out_shape=jax.ShapeDtypeStruct((B, T, ATTN_WIDTH), F32),
        scratch_shapes=[pltpu.VMEM((nc, kc, ATTN_WIDTH), BF16),
                        pltpu.VMEM((nc, ATTN_WIDTH, kc), BF16),
                        pltpu.VMEM((nc, kc, LANES), BF16),
                        pltpu.VMEM((nc, kc, ql), I32),
                        pltpu.VMEM((nc, kc, ql), F32),
                        pltpu.VMEM((ATTN_WIDTH, ql), F32)],
        compiler_params=pltpu.CompilerParams(dimension_semantics=("arbitrary", "arbitrary"),
                                             vmem_limit_bytes=VMEM_LIMIT),
        name="attend",
    )(*args)


def _merge_kernel(o_ref, sza_ref, sga_ref, gp_ref, x_ref, wao_ref, wo_ref, fg_ref, y_ref, *, final):
    a = (o_ref[...] * sza_ref[...]).astype(BF16)
    br_attn = jnp.dot(a, wao_ref[...], preferred_element_type=F32)
    merged = gp_ref[...] + sga_ref[...] * br_attn
    y = x_ref[...] + jnp.dot(merged.astype(BF16), wo_ref[...], preferred_element_type=F32)
    if final:
        r = lax.rsqrt(jnp.mean(y * y, axis=-1, keepdims=True) + EPS)
        y = y * r * fg_ref[...]
    y_ref[...] = y


def _merge(o, sza, sga, gp, x, w_ao, w_o, fg, *, final, tm):
    N = x.shape[0]
    const = dict(pipeline_mode=pl.Buffered(1))
    row = lambda n: pl.BlockSpec((tm, n), lambda t: (t, 0))
    return pl.pallas_call(
        functools.partial(_merge_kernel, final=final),
        grid=(N // tm,),
        in_specs=[row(ATTN_WIDTH), row(ATTN_WIDTH), row(D_MODEL), row(D_MODEL), row(D_MODEL),
                  pl.BlockSpec((ATTN_WIDTH, D_MODEL), lambda t: (0, 0), **const),
                  pl.BlockSpec((D_MODEL, D_MODEL), lambda t: (0, 0), **const),
                  pl.BlockSpec((1, D_MODEL), lambda t: (0, 0), **const)],
        out_specs=row(D_MODEL),
        out_shape=jax.ShapeDtypeStruct((N, D_MODEL), F32),
        compiler_params=pltpu.CompilerParams(dimension_semantics=("arbitrary",), vmem_limit_bytes=VMEM_LIMIT),
        name="merge",
    )(o, sza, sga, gp, x, w_ao, w_o, fg)


def _layer(x, left16, tabs, past_kv, wts, fg, *, past, final, tm, qb, kc):
    g, w_in, w_mix, pscale, w_po, w_ao, w_o = wts
    B, T, _ = x.shape
    gp, q, k, v, qi, slab, sza, sga, pstate = _proj(x, left16, tabs, g, w_in, w_mix, pscale, w_po, past=past, tm=tm)
    o = _attend(q, qi, slab, k, v, past_kv, qb=qb, kc=kc)
    flat = lambda a: a.reshape(B * T, a.shape[-1])
    y = _merge(flat(o), flat(sza), flat(sga), flat(gp), flat(x), w_ao, w_o, fg, final=final, tm=min(256, B * T))
    return (y.reshape(B, T, D_MODEL), k.reshape(B, T, N_HEADS, HEAD_DIM), v.reshape(B, T, N_HEADS, HEAD_DIM),
            slab[:, :, :IDX_DIM], pstate[:, 1:, :])


def kernel(x_prompt, x_sample, cache_k, cache_v, cache_kidx, state_pool, norm_g, w_in, w_pool_mix, pool_scale,
           w_pool_out, w_attn_out, w_o, final_norm_g):
    Bp, Tp, _ = x_prompt.shape
    Bs, Ts, _ = x_sample.shape
    depth = w_in.shape[0]
    past = cache_k.shape[2]
    tabs_p = _rope_tables(jnp.arange(Tp, dtype=jnp.int32))
    tabs_s = _rope_tables(past + jnp.arange(Ts, dtype=jnp.int32))
    left_p = jnp.zeros((Bp, HALO, POOL_WIDTH), F32)
    fg = final_norm_g.reshape(1, D_MODEL)
    tm_p = min(256, Tp)
    qb_p = min(256, Tp)

    xp, xs = x_prompt, x_sample
    outs_p, outs_s = [], []
    for l in range(depth):
        wts = (norm_g[l].reshape(1, D_MODEL), _prep_w_in(w_in[l]), w_pool_mix[l].astype(BF16),
               pool_scale[l].reshape(1, POOL_WIDTH), w_pool_out[l].astype(BF16), w_attn_out[l].astype(BF16),
               w_o[l].astype(BF16))
        final = l == depth - 1
        xp, *rest_p = _layer(xp, left_p, tabs_p, None, wts, fg, past=0, final=final, tm=tm_p, qb=qb_p, kc=qb_p)
        left_s = jnp.pad(state_pool[l], ((0, 0), (HALO - POOL_STATE, 0), (0, 0)))
        past_kv = (cache_k[l].reshape(Bs, past, ATTN_WIDTH), cache_v[l].reshape(Bs, past, ATTN_WIDTH), cache_kidx[l])
        xs, *rest_s = _layer(xs, left_s, tabs_s, past_kv, wts, fg, past=past, final=final, tm=Ts, qb=Ts, kc=256)
        outs_p.append(rest_p)
        outs_s.append(rest_s)
    stack = lambda outs, i: jnp.stack([o[i] for o in outs])
    return (xp, xs,
            stack(outs_p, 0), stack(outs_p, 1), stack(outs_p, 2), stack(outs_p, 3),
            stack(outs_s, 0), stack(outs_s, 1), stack(outs_s, 2), stack(outs_s, 3))
```

```python
import functools

import jax
import jax.numpy as jnp
import numpy as np
from jax import lax
from jax.experimental import pallas as pl
from jax.experimental.pallas import tpu as pltpu

F32 = jnp.float32
BF16 = jnp.bfloat16
I32 = jnp.int32

D_MODEL = 1024
CHUNK = 64
POOL_WIDTH = 512
POOL_WINDOWS = (2, 4, 8, 16)
POOL_GROUP_WIDTH = 128
POOL_STATE = 15
N_HEADS = 8
HEAD_DIM = 64
ATTN_WIDTH = N_HEADS * HEAD_DIM
IDX_HEADS = 8
IDX_DIM = 32
IDX_WIDTH = IDX_HEADS * IDX_DIM
TOPK_MAX = 256
ROPE_THETA = 10000.0
EPS = 1e-6

LANES = 128
HALO = 16
INT_MIN = -2147483648
NEG_BIG = -1e30
VMEM_LIMIT = 56 * 1024 * 1024

SEG_U, SEG_ZP, SEG_Q, SEG_K, SEG_V = 0, 512, 1024, 1536, 2048
SEG_QI, SEG_ZA, SEG_GP, SEG_GA, SEG_MISC = 2560, 2816, 3328, 4352, 5376
PROJ_COLS = 5504
_ORIG = np.cumsum([0, 512, 512, 512, 512, 512, 256, 32, 8, 512, 1024, 1024])


def _prep_w_in(w):
    o = _ORIG
    main = [w[:, o[0]:o[6]], w[:, o[8]:o[11]]]
    misc = jnp.concatenate([w[:, o[6]:o[8]], jnp.zeros((w.shape[0], LANES - IDX_DIM - IDX_HEADS), w.dtype)], axis=1)
    return jnp.concatenate(main + [misc], axis=1).astype(BF16)


def _rope_tables(pos):
    lane = jnp.arange(LANES)

    def tabs(d, passthrough_from=None):
        half = d // 2
        inv = ROPE_THETA ** (-jnp.arange(0, d, 2, dtype=F32) / d)
        ang = pos.astype(F32)[:, None] * inv[None, :]
        cos = jnp.cos(ang)[:, lane % half]
        sin = jnp.sin(ang)[:, lane % half]
        lo = (lane % d) < half
        sin_lo = jnp.where(lo[None, :], -sin, 0.0)
        sin_hi = jnp.where(lo[None, :], 0.0, sin)
        if passthrough_from is not None:
            keep = (lane < passthrough_from)[None, :]
            cos = jnp.where(keep, cos, 1.0)
            sin_lo = jnp.where(keep, sin_lo, 0.0)
            sin_hi = jnp.where(keep, sin_hi, 0.0)
        return [cos, sin_lo, sin_hi]

    return jnp.stack(tabs(HEAD_DIM) + tabs(IDX_DIM) + tabs(IDX_DIM, IDX_DIM)).astype(F32)


def _rope(x, cos, sin_lo, sin_hi, half):
    return x * cos + pltpu.roll(x, LANES - half, 1) * sin_lo + pltpu.roll(x, half, 1) * sin_hi


def _silu(x):
    return x * jax.nn.sigmoid(x)


def _pad_rows(a, rows):
    if a.shape[0] == rows:
        return a
    return jnp.concatenate([a, jnp.zeros((rows - a.shape[0],) + a.shape[1:], a.dtype)], axis=0)


def _transpose(a):
    r = a.shape[0]
    rp = -(-r // LANES) * LANES
    at = _pad_rows(a, rp).T
    return at if rp == r else at[:, :r]


def _proj_kernel(*refs, tm, past, kv_t, n_alias):
    (x_ref, left_ref, tab_ref, g_ref, win_ref, wmix_ref, pscale_ref, wpo_ref) = refs[:8]
    (gp_ref, q_ref, kb_ref, k_ref, v_ref, qi_ref, slab_ref, kit_ref, sza_ref, sga_ref, pstate_ref,
     h_scr, u_scr, pg_scr) = refs[8 + n_alias:]
    t = pl.program_id(1)
    x = x_ref[0]
    r = lax.rsqrt(jnp.mean(x * x, axis=-1, keepdims=True) + EPS)
    h_scr[...] = (x * r * g_ref[...]).astype(BF16)

    def seg(a, n):
        return jnp.dot(h_scr[...], win_ref[:, a:a + n], preferred_element_type=F32)

    @pl.when(t == 0)
    def _():
        u_scr[0:HALO, :] = left_ref[0]

    u_scr[HALO:HALO + tm, :] = seg(SEG_U, POOL_WIDTH)
    pos = past + t * tm + lax.broadcasted_iota(I32, (tm, 1), 0)
    zp = seg(SEG_ZP, POOL_WIDTH)
    for gi, w in enumerate(POOL_WINDOWS):
        cols = slice(gi * POOL_GROUP_WIDTH, (gi + 1) * POOL_GROUP_WIDTH)
        u_g = u_scr[HALO:HALO + tm, cols]
        s = u_g
        for j in range(1, w):
            s = s + u_scr[HALO - j:HALO - j + tm, cols]
        cnt = jnp.minimum(pos + 1, w).astype(F32)
        pooled = s / cnt - u_g
        mixed = jnp.dot(pooled.astype(BF16), wmix_ref[gi], preferred_element_type=F32) * pscale_ref[:, cols]
        pg_scr[:, cols] = (mixed * _silu(zp[:, cols])).astype(BF16)
    br_pool = jnp.dot(pg_scr[...], wpo_ref[...], preferred_element_type=F32)
    gp_ref[0] = jax.nn.sigmoid(seg(SEG_GP, D_MODEL)) * br_pool
    pstate_ref[0, 0] = u_scr[tm:tm + HALO, :]
    u_scr[0:HALO, :] = u_scr[tm:tm + HALO, :]

    c64, lo64, hi64 = tab_ref[0], tab_ref[1], tab_ref[2]
    c32, lo32, hi32 = tab_ref[3], tab_ref[4], tab_ref[5]
    cm, lom, him = tab_ref[6], tab_ref[7], tab_ref[8]
    qs = seg(SEG_Q, ATTN_WIDTH)
    ks = seg(SEG_K, ATTN_WIDTH)
    for c in range(ATTN_WIDTH // LANES):
        cols = slice(c * LANES, (c + 1) * LANES)
        q_ref[0, :, cols] = (_rope(qs[:, cols], c64, lo64, hi64, HEAD_DIM // 2) * HEAD_DIM ** -0.5).astype(BF16)
        kr = _rope(ks[:, cols], c64, lo64, hi64, HEAD_DIM // 2)
        kb_ref[0, :, cols] = kr.astype(BF16)
        if kv_t:
            k_ref[0, 0, cols, :] = _transpose(kr)
        else:
            k_ref[0, 0, :, cols] = kr
    v = seg(SEG_V, ATTN_WIDTH)
    v_ref[0, 0] = _transpose(v) if kv_t else v
    qis = seg(SEG_QI, IDX_WIDTH)
    for c in range(IDX_WIDTH // LANES):
        cols = slice(c * LANES, (c + 1) * LANES)
        qi_ref[0, :, cols] = _rope(qis[:, cols], c32, lo32, hi32, IDX_DIM // 2).astype(BF16)
    slab = _rope(seg(SEG_MISC, LANES), cm, lom, him, IDX_DIM // 2)
    slab_ref[0] = slab
    kit_ref[0, 0] = _transpose(slab)[0:IDX_DIM, :]
    sza_ref[0] = _silu(seg(SEG_ZA, ATTN_WIDTH))
    sga_ref[0] = jax.nn.sigmoid(seg(SEG_GA, D_MODEL))


def _proj(x, left16, tabs, g, w_in, w_mix, pscale, w_po, prev, *, layer, depth, past, tm, kv_t):
    B, T, _ = x.shape
    nt = T // tm
    const = dict(pipeline_mode=pl.Buffered(1))
    row = lambda n: pl.BlockSpec((1, tm, n), lambda b, t: (b, t, 0))
    if kv_t:
        kv_shape = (depth, B, ATTN_WIDTH, T)
        kv_spec = pl.BlockSpec((1, 1, ATTN_WIDTH, tm), lambda b, t: (layer, b, 0, t))
    else:
        kv_shape = (depth, B, T, ATTN_WIDTH)
        kv_spec = pl.BlockSpec((1, 1, tm, ATTN_WIDTH), lambda b, t: (layer, b, t, 0))
    out_shape = [
        jax.ShapeDtypeStruct((B, T, D_MODEL), F32),
        jax.ShapeDtypeStruct((B, T, ATTN_WIDTH), BF16),
        jax.ShapeDtypeStruct((B, T, ATTN_WIDTH), BF16),
        jax.ShapeDtypeStruct(kv_shape, F32),
        jax.ShapeDtypeStruct(kv_shape, F32),
        jax.ShapeDtypeStruct((B, T, IDX_WIDTH), BF16),
        jax.ShapeDtypeStruct((B, T, LANES), F32),
        jax.ShapeDtypeStruct((depth, B, IDX_DIM, T), F32),
        jax.ShapeDtypeStruct((B, T, ATTN_WIDTH), F32),
        jax.ShapeDtypeStruct((B, T, D_MODEL), F32),
        jax.ShapeDtypeStruct((depth, B, HALO, POOL_WIDTH), F32),
    ]
    out_specs = [row(D_MODEL), row(ATTN_WIDTH), row(ATTN_WIDTH), kv_spec, kv_spec, row(IDX_WIDTH), row(LANES),
                 pl.BlockSpec((1, 1, IDX_DIM, tm), lambda b, t: (layer, b, 0, t)),
                 row(ATTN_WIDTH), row(D_MODEL),
                 pl.BlockSpec((1, 1, HALO, POOL_WIDTH), lambda b, t: (layer, b, 0, 0))]
    in_specs = [
        row(D_MODEL),
        pl.BlockSpec((1, HALO, POOL_WIDTH), lambda b, t: (b, 0, 0)),
        pl.BlockSpec((9, tm, LANES), lambda b, t: (0, t, 0)),
        pl.BlockSpec((1, D_MODEL), lambda b, t: (0, 0), **const),
        pl.BlockSpec((D_MODEL, PROJ_COLS), lambda b, t: (0, 0), **const),
        pl.BlockSpec((4, POOL_GROUP_WIDTH, POOL_GROUP_WIDTH), lambda b, t: (0, 0, 0), **const),
        pl.BlockSpec((1, POOL_WIDTH), lambda b, t: (0, 0), **const),
        pl.BlockSpec((POOL_WIDTH, D_MODEL), lambda b, t: (0, 0), **const),
    ]
    args = [x, left16, tabs, g, w_in, w_mix, pscale, w_po]
    aliases = {}
    if prev is not None:
        for arr, out_idx in zip(prev, (3, 4, 7, 10)):
            aliases[len(args)] = out_idx
            in_specs.append(pl.BlockSpec(memory_space=pl.ANY))
            args.append(arr)
    return pl.pallas_call(
        functools.partial(_proj_kernel, tm=tm, past=past, kv_t=kv_t, n_alias=len(aliases)),
        grid=(B, nt),
        in_specs=in_specs,
        out_specs=out_specs,
        out_shape=out_shape,
        input_output_aliases=aliases,
        scratch_shapes=[pltpu.VMEM((tm, D_MODEL), BF16),
                        pltpu.VMEM((tm + HALO, POOL_WIDTH), F32),
                        pltpu.VMEM((tm, POOL_WIDTH), BF16)],
        compiler_params=pltpu.CompilerParams(dimension_semantics=("arbitrary", "arbitrary"),
                                             vmem_limit_bytes=VMEM_LIMIT),
        name="proj",
    )(*args)


def _attend_kernel(*refs, qb, ql, kc, n_past, past, total, ksel, has_past, kv_t):
    if has_past:
        (q_ref, qi_ref, slab_ref, k_ref, v_ref, pk_ref, pv_ref, pki_ref, rep_ref, rept_ref, tri_ref) = refs[:11]
        rest = refs[11:]
    else:
        (q_ref, qi_ref, slab_ref, k_ref, v_ref, rep_ref, rept_ref, tri_ref) = refs[:8]
        rest = refs[8:]
    (o_ref, kbuf, vtbuf, kibuf, key_scr, bias_scr, ot_scr, qm_scr, qim_scr, m_scr, l_scr, a_scr, s_scr, p_scr) = rest
    s = pl.program_id(1)
    contract_last = (((1,), (1,)), ((), ()))

    if has_past:
        @pl.when(s < n_past)
        def _():
            kbuf[s] = pk_ref[0, 0].T.astype(BF16)
            vtbuf[s] = pv_ref[0, 0].astype(BF16)
            ki_t = jnp.dot(rept_ref[...], pki_ref[0, 0].astype(BF16), preferred_element_type=F32)
            kibuf[s] = ki_t.T.astype(BF16)

    @pl.when(s >= n_past)
    def _():
        i = s - n_past
        c = n_past + i
        slab = _pad_rows(slab_ref[0], kc)
        kbuf[c] = _pad_rows(k_ref[0], kc)
        if kv_t:
            vtbuf[c] = v_ref[0, 0].astype(BF16)
        else:
            vtbuf[c] = _pad_rows(v_ref[0, 0], kc).T.astype(BF16)
        kibuf[c] = jnp.dot(slab.astype(BF16), rep_ref[...], preferred_element_type=F32).astype(BF16)

        w_t = _pad_rows(slab_ref[0], ql).T[IDX_DIM:IDX_DIM + IDX_HEADS, :] * (IDX_DIM ** -0.5 * IDX_HEADS ** -0.5)
        qi = _pad_rows(qi_ref[0], ql)
        q = _pad_rows(q_ref[0], ql)
        lane = lax.broadcasted_iota(I32, (ql, LANES), 1)
        heads_per_tile = LANES // IDX_DIM
        zero = jnp.zeros((), BF16)
        for h in range(IDX_HEADS):
            tile = h // heads_per_tile
            qim_scr[h] = jnp.where(lane // IDX_DIM == h % heads_per_tile, qi[:, tile * LANES:(tile + 1) * LANES], zero)
        for h in range(N_HEADS):
            pair = h // 2
            qm_scr[h] = jnp.where(lane // HEAD_DIM == h % 2, q[:, pair * LANES:(pair + 1) * LANES], zero)

        def score_keys(j):
            ki = kibuf[j]
            acc = jnp.zeros((kc, ql), F32)
            for h in range(IDX_HEADS):
                sh = lax.dot_general(ki, qim_scr[h], contract_last, preferred_element_type=F32)
                acc = acc + w_t[h:h + 1, :] * jnp.maximum(sh, 0.0)
            bits = pltpu.bitcast(acc, I32)
            key = bits ^ ((bits >> 31) & 0x7FFFFFFF)
            return jnp.where(acc == 0.0, 0, key)

        def full_chunk(j, carry):
            key_scr[j] = score_keys(j)
            return carry

        lax.fori_loop(0, c, full_chunk, 0)
        kpos = c * kc + lax.broadcasted_iota(I32, (kc, ql), 0)
        qpos = past + i * qb + lax.broadcasted_iota(I32, (kc, ql), 1)
        adm = (kpos // CHUNK <= qpos // CHUNK) & (kpos < total)
        key_scr[c] = jnp.where(adm, score_keys(c), INT_MIN)

        def count(cand, strict):
            def body(j, acc):
                kj = key_scr[j]
                hit = (kj > cand) if strict else (kj >= cand)
                return acc + jnp.where(hit, 1.0, 0.0).reshape(kc // 8, 8, ql).sum(axis=0)
            acc = lax.fori_loop(0, c + 1, body, jnp.zeros((8, ql), F32))
            return acc.sum(axis=0, keepdims=True)

        def bit_step(p, prefix):
            cand = prefix ^ (jnp.int32(1) << (31 - p))
            return jnp.where(count(cand, False) >= ksel, cand, prefix)

        thr = lax.fori_loop(0, 32, bit_step, jnp.full((1, ql), INT_MIN, I32))
        need = ksel - count(thr, True)
        live = thr != INT_MIN

        def select_chunk(j, seen):
            kj = key_scr[j]
            eq = kj == thr
            eqf = jnp.where(eq, 1.0, 0.0)
            rank = seen + jnp.dot(tri_ref[...], eqf.astype(BF16), preferred_element_type=F32)
            sel = (kj > thr) | (eq & (rank < need) & live)
            bias_scr[j] = jnp.where(sel, 0.0, NEG_BIG)
            return seen + eqf.reshape(kc // 8, 8, ql).sum(axis=0).sum(axis=0, keepdims=True)

        lax.fori_loop(0, c + 1, select_chunk, jnp.zeros((1, ql), F32))

        m_scr[...] = jnp.full((N_HEADS, ql), NEG_BIG, F32)
        l_scr[...] = jnp.zeros((N_HEADS, ql), F32)
        ot_scr[...] = jnp.zeros((ATTN_WIDTH, ql), F32)

        def attend_chunk(j, carry):
            for h in range(N_HEADS):
                pair = h // 2
                kj = kbuf[j, :, pair * LANES:(pair + 1) * LANES]
                sc = lax.dot_general(kj, qm_scr[h], contract_last, preferred_element_type=F32) + bias_scr[j]
                s_scr[h] = sc
                mx = sc.reshape(kc // 8, 8, ql).max(axis=0).max(axis=0, keepdims=True)
                m = m_scr[h:h + 1, :]
                m_new = jnp.maximum(m, mx)
                a_scr[h:h + 1, :] = jnp.exp(m - m_new)
                m_scr[h:h + 1, :] = m_new
            for h in range(N_HEADS):
                p = jnp.exp(s_scr[h] - m_scr[h:h + 1, :])
                p_scr[h] = p.astype(BF16)
                psum = p.reshape(kc // 8, 8, ql).sum(axis=0).sum(axis=0, keepdims=True)
                l_scr[h:h + 1, :] = a_scr[h:h + 1, :] * l_scr[h:h + 1, :] + psum
            for h in range(N_HEADS):
                rows = slice(h * HEAD_DIM, (h + 1) * HEAD_DIM)
                ot_scr[rows, :] = a_scr[h:h + 1, :] * ot_scr[rows, :] + jnp.dot(
                    vtbuf[j, rows, :], p_scr[h], preferred_element_type=F32)
            return carry

        lax.fori_loop(0, c + 1, attend_chunk, 0)
        for h in range(N_HEADS):
            rows = slice(h * HEAD_DIM, (h + 1) * HEAD_DIM)
            ot_scr[rows, :] = ot_scr[rows, :] / l_scr[h:h + 1, :]
        o_ref[0] = ot_scr[...].T[0:qb, :]


def _attend(q, qi, slab, kb, v_all, past_t, *, layer, qb, kc, kv_t):
    B, T, _ = q.shape
    has_past = past_t is not None
    past = past_t[0].shape[3] if has_past else 0
    total = past + T
    assert past % kc == 0 and T % qb == 0 and (qb == kc or T == qb)
    n_past = past // kc
    nb = T // qb
    nc = n_past + (T + kc - 1) // kc
    ql = max(qb, LANES)
    ksel = min(TOPK_MAX, total // 4)
    lane = jnp.arange(LANES)
    rep = (lane[:, None] == (lane[None, :] % IDX_DIM)) & (lane[:, None] < IDX_DIM)
    tri = jnp.arange(kc)[:, None] > jnp.arange(kc)[None, :]

    blk = lambda s: jnp.maximum(s - n_past, 0)
    cur = lambda n: pl.BlockSpec((1, qb, n), lambda b, s: (b, blk(s), 0))
    if kv_t:
        v_spec = pl.BlockSpec((1, 1, ATTN_WIDTH, qb), lambda b, s: (layer, b, 0, blk(s)))
    else:
        v_spec = pl.BlockSpec((1, 1, qb, ATTN_WIDTH), lambda b, s: (layer, b, blk(s), 0))
    in_specs = [cur(ATTN_WIDTH), cur(IDX_WIDTH), cur(LANES), cur(ATTN_WIDTH), v_spec]
    args = [q, qi, slab, kb, v_all]
    if has_past:
        old = lambda n: pl.BlockSpec((1, 1, n, kc), lambda b, s: (layer, b, 0, jnp.minimum(s, n_past - 1)))
        in_specs += [old(ATTN_WIDTH), old(ATTN_WIDTH), old(IDX_DIM)]
        args += list(past_t)
    in_specs += [pl.BlockSpec((LANES, LANES), lambda b, s: (0, 0)),
                 pl.BlockSpec((LANES, IDX_DIM), lambda b, s: (0, 0)),
                 pl.BlockSpec((kc, kc), lambda b, s: (0, 0))]
    args += [rep.astype(BF16), rep[:IDX_DIM, :].T.astype(BF16), tri.astype(BF16)]
    return pl.pallas_call(
        functools.partial(_attend_kernel, qb=qb, ql=ql, kc=kc, n_past=n_past, past=past, total=total,
                          ksel=ksel, has_past=has_past, kv_t=kv_t),
        grid=(B, n_past + nb),
        in_specs=in_specs,
        out_specs=cur(ATTN_WIDTH),
        out_shape=jax.ShapeDtypeStruct((B, T, ATTN_WIDTH), F32),
        scratch_shapes=[pltpu.VMEM((nc, kc, ATTN_WIDTH), BF16),
                        pltpu.VMEM((nc, ATTN_WIDTH, kc), BF16),
                        pltpu.VMEM((nc, kc, LANES), BF16),
                        pltpu.VMEM((nc, kc, ql), I32),
                        pltpu.VMEM((nc, kc, ql), F32),
                        pltpu.VMEM((ATTN_WIDTH, ql), F32),
                        pltpu.VMEM((N_HEADS, ql, LANES), BF16),
                        pltpu.VMEM((IDX_HEADS, ql, LANES), BF16),
                        pltpu.VMEM((N_HEADS, ql), F32),
                        pltpu.VMEM((N_HEADS, ql), F32),
                        pltpu.VMEM((N_HEADS, ql), F32),
                        pltpu.VMEM((N_HEADS, kc, ql), F32),
                        pltpu.VMEM((N_HEADS, kc, ql), BF16)],
        compiler_params=pltpu.CompilerParams(dimension_semantics=("arbitrary", "arbitrary"),
                                             vmem_limit_bytes=VMEM_LIMIT),
        name="attend",
    )(*args)


def _merge_kernel(o_ref, sza_ref, sga_ref, gp_ref, x_ref, wao_ref, wo_ref, fg_ref, y_ref, *, final):
    a = (o_ref[...] * sza_ref[...]).astype(BF16)
    br_attn = jnp.dot(a, wao_ref[...], preferred_element_type=F32)
    merged = gp_ref[...] + sga_ref[...] * br_attn
    y = x_ref[...] + jnp.dot(merged.astype(BF16), wo_ref[...], preferred_element_type=F32)
    if final:
        r = lax.rsqrt(jnp.mean(y * y, axis=-1, keepdims=True) + EPS)
        y = y * r * fg_ref[...]
    y_ref[...] = y


def _merge(o, sza, sga, gp, x, w_ao, w_o, fg, *, final, tm):
    N = x.shape[0]
    const = dict(pipeline_mode=pl.Buffered(1))
    row = lambda n: pl.BlockSpec((tm, n), lambda t: (t, 0))
    return pl.pallas_call(
        functools.partial(_merge_kernel, final=final),
        grid=(N // tm,),
        in_specs=[row(ATTN_WIDTH), row(ATTN_WIDTH), row(D_MODEL), row(D_MODEL), row(D_MODEL),
                  pl.BlockSpec((ATTN_WIDTH, D_MODEL), lambda t: (0, 0), **const),
                  pl.BlockSpec((D_MODEL, D_MODEL), lambda t: (0, 0), **const),
                  pl.BlockSpec((1, D_MODEL), lambda t: (0, 0), **const)],
        out_specs=row(D_MODEL),
        out_shape=jax.ShapeDtypeStruct((N, D_MODEL), F32),
        compiler_params=pltpu.CompilerParams(dimension_semantics=("arbitrary",), vmem_limit_bytes=VMEM_LIMIT),
        name="merge",
    )(o, sza, sga, gp, x, w_ao, w_o, fg)


def _layer(x, left16, tabs, past_t, wts, fg, prev, *, layer, depth, past, tm, qb, kc, kv_t):
    g, w_in, w_mix, pscale, w_po, w_ao, w_o = wts
    B, T, _ = x.shape
    gp, q, kb, k_all, v_all, qi, slab, kit_all, sza, sga, pstate_all = _proj(
        x, left16, tabs, g, w_in, w_mix, pscale, w_po, prev, layer=layer, depth=depth, past=past, tm=tm, kv_t=kv_t)
    o = _attend(q, qi, slab, kb, v_all, past_t, layer=layer, qb=qb, kc=kc, kv_t=kv_t)
    flat = lambda a: a.reshape(B * T, a.shape[-1])
    y = _merge(flat(o), flat(sza), flat(sga), flat(gp), flat(x), w_ao, w_o, fg,
               final=layer == depth - 1, tm=min(256, B * T))
    return y.reshape(B, T, D_MODEL), (k_all, v_all, kit_all, pstate_all)


def kernel(x_prompt, x_sample, cache_k, cache_v, cache_kidx, state_pool, norm_g, w_in, w_pool_mix, pool_scale,
           w_pool_out, w_attn_out, w_o, final_norm_g):
    Bp, Tp, _ = x_prompt.shape
    Bs, Ts, _ = x_sample.shape
    depth = w_in.shape[0]
    past = cache_k.shape[2]
    tabs_p = _rope_tables(jnp.arange(Tp, dtype=jnp.int32))
    tabs_s = _rope_tables(past + jnp.arange(Ts, dtype=jnp.int32))
    left_p = jnp.zeros((Bp, HALO, POOL_WIDTH), F32)
    fg = final_norm_g.reshape(1, D_MODEL)
    tm_p = min(256, Tp)
    past_t = (jnp.transpose(cache_k, (0, 1, 3, 4, 2)).reshape(depth, Bs, ATTN_WIDTH, past),
              jnp.transpose(cache_v, (0, 1, 3, 4, 2)).reshape(depth, Bs, ATTN_WIDTH, past),
              jnp.transpose(cache_kidx, (0, 1, 3, 2)))

    xp, xs = x_prompt, x_sample
    prev_p = prev_s = None
    for l in range(depth):
        wts = (norm_g[l].reshape(1, D_MODEL), _prep_w_in(w_in[l]), w_pool_mix[l].astype(BF16),
               pool_scale[l].reshape(1, POOL_WIDTH), w_pool_out[l].astype(BF16), w_attn_out[l].astype(BF16),
               w_o[l].astype(BF16))
        xp, prev_p = _layer(xp, left_p, tabs_p, None, wts, fg, prev_p, layer=l, depth=depth, past=0,
                            tm=tm_p, qb=tm_p, kc=tm_p, kv_t=True)
        left_s = jnp.pad(state_pool[l], ((0, 0), (HALO - POOL_STATE, 0), (0, 0)))
        xs, prev_s = _layer(xs, left_s, tabs_s, past_t, wts, fg, prev_s, layer=l, depth=depth, past=past,
                            tm=Ts, qb=Ts, kc=256, kv_t=False)
    kp, vp, kitp, psp = prev_p
    ks, vs, kits, pss = prev_s
    heads_t = lambda a: jnp.transpose(a.reshape(depth, Bp, N_HEADS, HEAD_DIM, Tp), (0, 1, 4, 2, 3))
    heads = lambda a: a.reshape(depth, Bs, Ts, N_HEADS, HEAD_DIM)
    return (xp, xs,
            heads_t(kp), heads_t(vp), jnp.transpose(kitp, (0, 1, 3, 2)), psp[:, :, HALO - POOL_STATE:, :],
            heads(ks), heads(vs), jnp.transpose(kits, (0, 1, 3, 2)), pss[:, :, HALO - POOL_STATE:, :])
```

```python
import functools

import jax
import jax.numpy as jnp
import numpy as np
from jax import lax
from jax.experimental import pallas as pl
from jax.experimental.pallas import tpu as pltpu

F32 = jnp.float32
BF16 = jnp.bfloat16
I32 = jnp.int32

D_MODEL = 1024
CHUNK = 64
POOL_WIDTH = 512
POOL_WINDOWS = (2, 4, 8, 16)
POOL_GROUP_WIDTH = 128
POOL_STATE = 15
N_HEADS = 8
HEAD_DIM = 64
ATTN_WIDTH = N_HEADS * HEAD_DIM
IDX_HEADS = 8
IDX_DIM = 32
IDX_WIDTH = IDX_HEADS * IDX_DIM
TOPK_MAX = 256
ROPE_THETA = 10000.0
EPS = 1e-6

LANES = 128
PACK = 16
HALO = 16
INT_MIN = -2147483648
MIN_NORMAL_TOP16 = 0x0080
NEG_BIG = -1e30
LOG2E = 1.4426950408889634
Q_SCALE = HEAD_DIM ** -0.5 * LOG2E
VT_ROWS = HEAD_DIM + PACK
VMEM_LIMIT = 56 * 1024 * 1024

SEG_U, SEG_ZP, SEG_Q, SEG_K, SEG_V = 0, 512, 1024, 1536, 2048
SEG_QI, SEG_ZA, SEG_GP, SEG_GA, SEG_MISC = 2560, 2816, 3328, 4352, 5376
PROJ_COLS = 5504
_ORIG = np.cumsum([0, 512, 512, 512, 512, 512, 256, 32, 8, 512, 1024, 1024])


def _prep_w_in(w):
    o = _ORIG
    main = [w[:, o[0]:o[6]], w[:, o[8]:o[11]]]
    misc = jnp.concatenate([w[:, o[6]:o[8]], jnp.zeros((w.shape[0], LANES - IDX_DIM - IDX_HEADS), w.dtype)], axis=1)
    return jnp.concatenate(main + [misc], axis=1).astype(BF16)


def _rope_tables(pos):
    lane = jnp.arange(LANES)

    def tabs(d, passthrough_from=None):
        half = d // 2
        inv = ROPE_THETA ** (-jnp.arange(0, d, 2, dtype=F32) / d)
        ang = pos.astype(F32)[:, None] * inv[None, :]
        cos = jnp.cos(ang)[:, lane % half]
        sin = jnp.sin(ang)[:, lane % half]
        lo = (lane % d) < half
        sin_lo = jnp.where(lo[None, :], -sin, 0.0)
        sin_hi = jnp.where(lo[None, :], 0.0, sin)
        if passthrough_from is not None:
            keep = (lane < passthrough_from)[None, :]
            cos = jnp.where(keep, cos, 1.0)
            sin_lo = jnp.where(keep, sin_lo, 0.0)
            sin_hi = jnp.where(keep, sin_hi, 0.0)
        return [cos, sin_lo, sin_hi]

    return jnp.stack(tabs(HEAD_DIM) + tabs(IDX_DIM) + tabs(IDX_DIM, IDX_DIM)).astype(F32)


def _rope(x, cos, sin_lo, sin_hi, half):
    return x * cos + pltpu.roll(x, LANES - half, 1) * sin_lo + pltpu.roll(x, half, 1) * sin_hi


def _silu(x):
    return x * jax.nn.sigmoid(x)


def _pad_rows(a, rows):
    if a.shape[0] == rows:
        return a
    return jnp.concatenate([a, jnp.zeros((rows - a.shape[0],) + a.shape[1:], a.dtype)], axis=0)


def _transpose(a):
    r = a.shape[0]
    rp = -(-r // LANES) * LANES
    at = _pad_rows(a, rp).T
    return at if rp == r else at[:, :r]


def _proj_kernel(*refs, tm, past, kv_t, n_alias):
    (x_ref, left_ref, tab_ref, g_ref, win_ref, wmix_ref, pscale_ref, wpo_ref) = refs[:8]
    (gp_ref, q_ref, kb_ref, k_ref, v_ref, qi_ref, slab_ref, kit_ref, sza_ref, sga_ref, pstate_ref,
     h_scr, u_scr, pg_scr) = refs[8 + n_alias:]
    t = pl.program_id(1)
    x = x_ref[0]
    r = lax.rsqrt(jnp.mean(x * x, axis=-1, keepdims=True) + EPS)
    h_scr[...] = (x * r * g_ref[...]).astype(BF16)

    def seg(a, n):
        return jnp.dot(h_scr[...], win_ref[:, a:a + n], preferred_element_type=F32)

    @pl.when(t == 0)
    def _():
        u_scr[0:HALO, :] = left_ref[0]

    u_scr[HALO:HALO + tm, :] = seg(SEG_U, POOL_WIDTH)
    pos = past + t * tm + lax.broadcasted_iota(I32, (tm, 1), 0)
    zp = seg(SEG_ZP, POOL_WIDTH)
    for gi, w in enumerate(POOL_WINDOWS):
        cols = slice(gi * POOL_GROUP_WIDTH, (gi + 1) * POOL_GROUP_WIDTH)
        u_g = u_scr[HALO:HALO + tm, cols]
        s = u_g
        for j in range(1, w):
            s = s + u_scr[HALO - j:HALO - j + tm, cols]
        cnt = jnp.minimum(pos + 1, w).astype(F32)
        pooled = s / cnt - u_g
        mixed = jnp.dot(pooled.astype(BF16), wmix_ref[gi], preferred_element_type=F32) * pscale_ref[:, cols]
        pg_scr[:, cols] = (mixed * _silu(zp[:, cols])).astype(BF16)
    br_pool = jnp.dot(pg_scr[...], wpo_ref[...], preferred_element_type=F32)
    gp_ref[0] = jax.nn.sigmoid(seg(SEG_GP, D_MODEL)) * br_pool
    pstate_ref[0, 0] = u_scr[tm:tm + HALO, :]
    u_scr[0:HALO, :] = u_scr[tm:tm + HALO, :]

    c64, lo64, hi64 = tab_ref[0], tab_ref[1], tab_ref[2]
    c32, lo32, hi32 = tab_ref[3], tab_ref[4], tab_ref[5]
    cm, lom, him = tab_ref[6], tab_ref[7], tab_ref[8]
    qs = seg(SEG_Q, ATTN_WIDTH)
    ks = seg(SEG_K, ATTN_WIDTH)
    for c in range(ATTN_WIDTH // LANES):
        cols = slice(c * LANES, (c + 1) * LANES)
        q_ref[0, :, cols] = (_rope(qs[:, cols], c64, lo64, hi64, HEAD_DIM // 2) * Q_SCALE).astype(BF16)
        kr = _rope(ks[:, cols], c64, lo64, hi64, HEAD_DIM // 2)
        kb_ref[0, :, cols] = kr.astype(BF16)
        if kv_t:
            k_ref[0, 0, cols, :] = _transpose(kr)
        else:
            k_ref[0, 0, :, cols] = kr
    v = seg(SEG_V, ATTN_WIDTH)
    v_ref[0, 0] = _transpose(v) if kv_t else v
    qis = seg(SEG_QI, IDX_WIDTH)
    for c in range(IDX_WIDTH // LANES):
        cols = slice(c * LANES, (c + 1) * LANES)
        qi_ref[0, :, cols] = _rope(qis[:, cols], c32, lo32, hi32, IDX_DIM // 2).astype(BF16)
    slab = _rope(seg(SEG_MISC, LANES), cm, lom, him, IDX_DIM // 2)
    slab_ref[0] = slab
    kit_ref[0, 0] = _transpose(slab)[0:IDX_DIM, :]
    sza_ref[0] = _silu(seg(SEG_ZA, ATTN_WIDTH))
    sga_ref[0] = jax.nn.sigmoid(seg(SEG_GA, D_MODEL))


def _proj(x, left16, tabs, g, w_in, w_mix, pscale, w_po, prev, *, layer, depth, past, tm, kv_t):
    B, T, _ = x.shape
    nt = T // tm
    const = dict(pipeline_mode=pl.Buffered(1))
    row = lambda n: pl.BlockSpec((1, tm, n), lambda b, t: (b, t, 0))
    if kv_t:
        kv_shape = (depth, B, ATTN_WIDTH, T)
        kv_spec = pl.BlockSpec((1, 1, ATTN_WIDTH, tm), lambda b, t: (layer, b, 0, t))
    else:
        kv_shape = (depth, B, T, ATTN_WIDTH)
        kv_spec = pl.BlockSpec((1, 1, tm, ATTN_WIDTH), lambda b, t: (layer, b, t, 0))
    out_shape = [
        jax.ShapeDtypeStruct((B, T, D_MODEL), F32),
        jax.ShapeDtypeStruct((B, T, ATTN_WIDTH), BF16),
        jax.ShapeDtypeStruct((B, T, ATTN_WIDTH), BF16),
        jax.ShapeDtypeStruct(kv_shape, F32),
        jax.ShapeDtypeStruct(kv_shape, F32),
        jax.ShapeDtypeStruct((B, T, IDX_WIDTH), BF16),
        jax.ShapeDtypeStruct((B, T, LANES), F32),
        jax.ShapeDtypeStruct((depth, B, IDX_DIM, T), F32),
        jax.ShapeDtypeStruct((B, T, ATTN_WIDTH), F32),
        jax.ShapeDtypeStruct((B, T, D_MODEL), F32),
        jax.ShapeDtypeStruct((depth, B, HALO, POOL_WIDTH), F32),
    ]
    out_specs = [row(D_MODEL), row(ATTN_WIDTH), row(ATTN_WIDTH), kv_spec, kv_spec, row(IDX_WIDTH), row(LANES),
                 pl.BlockSpec((1, 1, IDX_DIM, tm), lambda b, t: (layer, b, 0, t)),
                 row(ATTN_WIDTH), row(D_MODEL),
                 pl.BlockSpec((1, 1, HALO, POOL_WIDTH), lambda b, t: (layer, b, 0, 0))]
    in_specs = [
        row(D_MODEL),
        pl.BlockSpec((1, HALO, POOL_WIDTH), lambda b, t: (b, 0, 0)),
        pl.BlockSpec((9, tm, LANES), lambda b, t: (0, t, 0)),
        pl.BlockSpec((1, D_MODEL), lambda b, t: (0, 0), **const),
        pl.BlockSpec((D_MODEL, PROJ_COLS), lambda b, t: (0, 0), **const),
        pl.BlockSpec((4, POOL_GROUP_WIDTH, POOL_GROUP_WIDTH), lambda b, t: (0, 0, 0), **const),
        pl.BlockSpec((1, POOL_WIDTH), lambda b, t: (0, 0), **const),
        pl.BlockSpec((POOL_WIDTH, D_MODEL), lambda b, t: (0, 0), **const),
    ]
    args = [x, left16, tabs, g, w_in, w_mix, pscale, w_po]
    aliases = {}
    if prev is not None:
        for arr, out_idx in zip(prev, (3, 4, 7, 10)):
            aliases[len(args)] = out_idx
            in_specs.append(pl.BlockSpec(memory_space=pl.ANY))
            args.append(arr)
    return pl.pallas_call(
        functools.partial(_proj_kernel, tm=tm, past=past, kv_t=kv_t, n_alias=len(aliases)),
        grid=(B, nt),
        in_specs=in_specs,
        out_specs=out_specs,
        out_shape=out_shape,
        input_output_aliases=aliases,
        scratch_shapes=[pltpu.VMEM((tm, D_MODEL), BF16),
                        pltpu.VMEM((tm + HALO, POOL_WIDTH), F32),
                        pltpu.VMEM((tm, POOL_WIDTH), BF16)],
        compiler_params=pltpu.CompilerParams(dimension_semantics=("arbitrary", "arbitrary"),
                                             vmem_limit_bytes=VMEM_LIMIT),
        name="proj",
    )(*args)


def _attend_kernel(*refs, qb, ql, kc, n_past, past, total, ksel, has_past, kv_t):
    if has_past:
        (q_ref, qi_ref, slab_ref, k_ref, v_ref, pk_ref, pv_ref, pki_ref, rep_ref, rept_ref, tri_ref) = refs[:11]
        rest = refs[11:]
    else:
        (q_ref, qi_ref, slab_ref, k_ref, v_ref, rep_ref, rept_ref, tri_ref) = refs[:8]
        rest = refs[8:]
    (o_ref, kbuf, vtbuf, kibuf, key_scr, rk_scr, bias_scr, ot_scr, on_scr, qm_scr, qim_scr, m_scr, a_scr, s_scr,
     p_scr) = rest
    s = pl.program_id(1)
    contract_last = (((1,), (1,)), ((), ()))

    def store_values(j, vt):
        for h in range(N_HEADS):
            vtbuf[j, h, 0:HEAD_DIM, :] = vt[h * HEAD_DIM:(h + 1) * HEAD_DIM, :].astype(BF16)
            vtbuf[j, h, HEAD_DIM:VT_ROWS, :] = jnp.ones((PACK, kc), BF16)

    if has_past:
        @pl.when(s < n_past)
        def _():
            kbuf[s] = pk_ref[0, 0].T.astype(BF16)
            store_values(s, pv_ref[0, 0])
            ki_t = jnp.dot(rept_ref[...], pki_ref[0, 0].astype(BF16), preferred_element_type=F32)
            kibuf[s] = ki_t.T.astype(BF16)

    @pl.when(s >= n_past)
    def _():
        i = s - n_past
        c = n_past + i
        slab = _pad_rows(slab_ref[0], kc)
        kbuf[c] = _pad_rows(k_ref[0], kc)
        store_values(c, v_ref[0, 0] if kv_t else _pad_rows(v_ref[0, 0], kc).T)
        kibuf[c] = jnp.dot(slab.astype(BF16), rep_ref[...], preferred_element_type=F32).astype(BF16)

        w_t = _pad_rows(slab_ref[0], ql).T[IDX_DIM:IDX_DIM + IDX_HEADS, :] * (IDX_DIM ** -0.5 * IDX_HEADS ** -0.5)
        qi = _pad_rows(qi_ref[0], ql)
        q = _pad_rows(q_ref[0], ql)
        lane = lax.broadcasted_iota(I32, (ql, LANES), 1)
        heads_per_tile = LANES // IDX_DIM
        zero = jnp.zeros((), BF16)
        for h in range(IDX_HEADS):
            tile = h // heads_per_tile
            qim_scr[h] = jnp.where(lane // IDX_DIM == h % heads_per_tile, qi[:, tile * LANES:(tile + 1) * LANES], zero)
        for h in range(N_HEADS):
            pair = h // 2
            qm_scr[h] = jnp.where(lane // HEAD_DIM == h % 2, q[:, pair * LANES:(pair + 1) * LANES], zero)

        def score_keys(j):
            ki = kibuf[j]
            acc = jnp.zeros((kc, ql), F32)
            for h in range(IDX_HEADS):
                sh = lax.dot_general(ki, qim_scr[h], contract_last, preferred_element_type=F32)
                acc = acc + w_t[h:h + 1, :] * jnp.maximum(sh, 0.0)
            bits = pltpu.bitcast(acc, I32)
            key = bits ^ ((bits >> 31) & 0x7FFFFFFF)
            return jnp.where(acc == 0.0, 0, key)

        def store_keys(j, key):
            key_scr[j] = key
            score_bits = key ^ ((key >> 31) & 0x7FFFFFFF)
            rk_scr[j] = pltpu.bitcast(score_bits & -65536, F32).astype(BF16)

        def full_chunk(j, carry):
            store_keys(j, score_keys(j))
            return carry

        lax.fori_loop(0, c, full_chunk, 0)
        kpos = c * kc + lax.broadcasted_iota(I32, (kc, ql), 0)
        qpos = past + i * qb + lax.broadcasted_iota(I32, (kc, ql), 1)
        adm = (kpos // CHUNK <= qpos // CHUNK) & (kpos < total)
        store_keys(c, jnp.where(adm, score_keys(c), INT_MIN))

        one, nil = jnp.ones((), BF16), jnp.zeros((), BF16)

        def count_packed(cand, strict):
            cb = jnp.broadcast_to(cand, (PACK, ql)).astype(BF16)[None]

            def body(j, acc):
                x = rk_scr[j].reshape(kc // PACK, PACK, ql)
                hit = (x > cb) if strict else (x >= cb)
                parts = list(jnp.where(hit, one, nil))
                while len(parts) > 1:
                    parts = [a + b for a, b in zip(parts[::2], parts[1::2])]
                return acc + parts[0].astype(F32)
            acc = lax.fori_loop(0, c + 1, body, jnp.zeros((PACK, ql), F32))
            return acc.sum(axis=0, keepdims=True)

        def top16_as_float(k16):
            pat = k16 ^ ((k16 >> 31) & 0x7FFF)
            pat = jnp.where((pat > 0) & (pat < MIN_NORMAL_TOP16), MIN_NORMAL_TOP16, pat)
            return pltpu.bitcast(pat << 16, F32)

        def digit_search(nbits, to_float, want):
            def step(p, pre):
                cand = pre | (jnp.int32(1) << (nbits - 1 - p))
                return jnp.where(count_packed(to_float(cand), False) >= want, cand, pre)
            return lax.fori_loop(0, nbits, step, jnp.zeros((1, ql), I32))

        def load_digit(shift, upper):
            def body(j, carry):
                kj = key_scr[j]
                d = jnp.where((kj >> (shift + 8)) == upper, (kj >> shift) & 0xFF, -1)
                rk_scr[j] = d.astype(F32).astype(BF16)
                return carry
            lax.fori_loop(0, c + 1, body, 0)

        half = 1 << 15
        hi = digit_search(16, lambda u: top16_as_float(u - half), ksel) - half
        want = ksel - count_packed(top16_as_float(hi), True)
        load_digit(8, hi)
        mid = digit_search(8, lambda d: d.astype(F32), want)
        want = want - count_packed(mid.astype(F32), True)
        upper = (hi << 8) | mid
        load_digit(0, upper)
        low = digit_search(8, lambda d: d.astype(F32), want)
        thr = (upper << 8) | low
        above = count_packed(low.astype(F32), True)
        ties = count_packed(low.astype(F32), False) - above
        need = want - above
        live = thr != INT_MIN
        real = lax.broadcasted_iota(I32, (1, ql), 1) < qb
        surplus = jnp.max(jnp.where(live & real & (ties > need), 1.0, 0.0)) > 0.0

        @pl.when(jnp.logical_not(surplus))
        def _():
            def select_chunk(j, carry):
                kj = key_scr[j]
                bias_scr[j] = jnp.where((kj > thr) | ((kj == thr) & live), 0.0, NEG_BIG)
                return carry
            lax.fori_loop(0, c + 1, select_chunk, 0)

        @pl.when(surplus)
        def _():
            def select_chunk(j, seen):
                kj = key_scr[j]
                eq = kj == thr
                eqf = jnp.where(eq, 1.0, 0.0)
                rank = seen + jnp.dot(tri_ref[...], eqf.astype(BF16), preferred_element_type=F32)
                sel = (kj > thr) | (eq & (rank < need) & live)
                bias_scr[j] = jnp.where(sel, 0.0, NEG_BIG)
                return seen + eqf.reshape(kc // 8, 8, ql).sum(axis=0).sum(axis=0, keepdims=True)
            lax.fori_loop(0, c + 1, select_chunk, jnp.zeros((1, ql), F32))

        m_scr[...] = jnp.full((N_HEADS, ql), NEG_BIG, F32)
        ot_scr[...] = jnp.zeros((N_HEADS, VT_ROWS, ql), F32)

        def attend_chunk(j, carry):
            for h in range(N_HEADS):
                pair = h // 2
                kj = kbuf[j, :, pair * LANES:(pair + 1) * LANES]
                sc = lax.dot_general(kj, qm_scr[h], contract_last, preferred_element_type=F32) + bias_scr[j]
                s_scr[h] = sc
                mx = sc.reshape(kc // 8, 8, ql).max(axis=0).max(axis=0, keepdims=True)
                m = m_scr[h:h + 1, :]
                m_new = jnp.maximum(m, mx)
                a_scr[h:h + 1, :] = jnp.exp2(m - m_new)
                m_scr[h:h + 1, :] = m_new
            for h in range(N_HEADS):
                p_scr[h] = jnp.exp2(s_scr[h] - m_scr[h:h + 1, :]).astype(BF16)
            for h in range(N_HEADS):
                ot_scr[h] = a_scr[h:h + 1, :] * ot_scr[h] + jnp.dot(vtbuf[j, h], p_scr[h],
                                                                    preferred_element_type=F32)
            return carry

        lax.fori_loop(0, c + 1, attend_chunk, 0)
        for h in range(N_HEADS):
            rows = slice(h * HEAD_DIM, (h + 1) * HEAD_DIM)
            on_scr[rows, :] = ot_scr[h, 0:HEAD_DIM, :] / ot_scr[h, HEAD_DIM:HEAD_DIM + 1, :]
        o_ref[0] = on_scr[...].T[0:qb, :]


def _attend(q, qi, slab, kb, v_all, past_t, *, layer, qb, kc, kv_t):
    B, T, _ = q.shape
    has_past = past_t is not None
    past = past_t[0].shape[3] if has_past else 0
    total = past + T
    assert past % kc == 0 and T % qb == 0 and (qb == kc or T == qb)
    n_past = past // kc
    nb = T // qb
    nc = n_past + (T + kc - 1) // kc
    ql = max(qb, LANES)
    ksel = min(TOPK_MAX, total // 4)
    lane = jnp.arange(LANES)
    rep = (lane[:, None] == (lane[None, :] % IDX_DIM)) & (lane[:, None] < IDX_DIM)
    tri = jnp.arange(kc)[:, None] > jnp.arange(kc)[None, :]

    blk = lambda s: jnp.maximum(s - n_past, 0)
    cur = lambda n: pl.BlockSpec((1, qb, n), lambda b, s: (b, blk(s), 0))
    if kv_t:
        v_spec = pl.BlockSpec((1, 1, ATTN_WIDTH, qb), lambda b, s: (layer, b, 0, blk(s)))
    else:
        v_spec = pl.BlockSpec((1, 1, qb, ATTN_WIDTH), lambda b, s: (layer, b, blk(s), 0))
    in_specs = [cur(ATTN_WIDTH), cur(IDX_WIDTH), cur(LANES), cur(ATTN_WIDTH), v_spec]
    args = [q, qi, slab, kb, v_all]
    if has_past:
        old = lambda n: pl.BlockSpec((1, 1, n, kc), lambda b, s: (layer, b, 0, jnp.minimum(s, n_past - 1)))
        in_specs += [old(ATTN_WIDTH), old(ATTN_WIDTH), old(IDX_DIM)]
        args += list(past_t)
    in_specs += [pl.BlockSpec((LANES, LANES), lambda b, s: (0, 0)),
                 pl.BlockSpec((LANES, IDX_DIM), lambda b, s: (0, 0)),
                 pl.BlockSpec((kc, kc), lambda b, s: (0, 0))]
    args += [rep.astype(BF16), rep[:IDX_DIM, :].T.astype(BF16), tri.astype(BF16)]
    return pl.pallas_call(
        functools.partial(_attend_kernel, qb=qb, ql=ql, kc=kc, n_past=n_past, past=past, total=total,
                          ksel=ksel, has_past=has_past, kv_t=kv_t),
        grid=(B, n_past + nb),
        in_specs=in_specs,
        out_specs=cur(ATTN_WIDTH),
        out_shape=jax.ShapeDtypeStruct((B, T, ATTN_WIDTH), F32),
        scratch_shapes=[pltpu.VMEM((nc, kc, ATTN_WIDTH), BF16),
                        pltpu.VMEM((nc, N_HEADS, VT_ROWS, kc), BF16),
                        pltpu.VMEM((nc, kc, LANES), BF16),
                        pltpu.VMEM((nc, kc, ql), I32),
                        pltpu.VMEM((nc, kc, ql), BF16),
                        pltpu.VMEM((nc, kc, ql), F32),
                        pltpu.VMEM((N_HEADS, VT_ROWS, ql), F32),
                        pltpu.VMEM((ATTN_WIDTH, ql), F32),
                        pltpu.VMEM((N_HEADS, ql, LANES), BF16),
                        pltpu.VMEM((IDX_HEADS, ql, LANES), BF16),
                        pltpu.VMEM((N_HEADS, ql), F32),
                        pltpu.VMEM((N_HEADS, ql), F32),
                        pltpu.VMEM((N_HEADS, kc, ql), F32),
                        pltpu.VMEM((N_HEADS, kc, ql), BF16)],
        compiler_params=pltpu.CompilerParams(dimension_semantics=("arbitrary", "arbitrary"),
                                             vmem_limit_bytes=VMEM_LIMIT),
        name="attend",
    )(*args)


def _merge_kernel(o_ref, sza_ref, sga_ref, gp_ref, x_ref, wao_ref, wo_ref, fg_ref, y_ref, *, final):
    a = (o_ref[...] * sza_ref[...]).astype(BF16)
    br_attn = jnp.dot(a, wao_ref[...], preferred_element_type=F32)
    merged = gp_ref[...] + sga_ref[...] * br_attn
    y = x_ref[...] + jnp.dot(merged.astype(BF16), wo_ref[...], preferred_element_type=F32)
    if final:
        r = lax.rsqrt(jnp.mean(y * y, axis=-1, keepdims=True) + EPS)
        y = y * r * fg_ref[...]
    y_ref[...] = y


def _merge(o, sza, sga, gp, x, w_ao, w_o, fg, *, final, tm):
    N = x.shape[0]
    const = dict(pipeline_mode=pl.Buffered(1))
    row = lambda n: pl.BlockSpec((tm, n), lambda t: (t, 0))
    return pl.pallas_call(
        functools.partial(_merge_kernel, final=final),
        grid=(N // tm,),
        in_specs=[row(ATTN_WIDTH), row(ATTN_WIDTH), row(D_MODEL), row(D_MODEL), row(D_MODEL),
                  pl.BlockSpec((ATTN_WIDTH, D_MODEL), lambda t: (0, 0), **const),
                  pl.BlockSpec((D_MODEL, D_MODEL), lambda t: (0, 0), **const),
                  pl.BlockSpec((1, D_MODEL), lambda t: (0, 0), **const)],
        out_specs=row(D_MODEL),
        out_shape=jax.ShapeDtypeStruct((N, D_MODEL), F32),
        compiler_params=pltpu.CompilerParams(dimension_semantics=("arbitrary",), vmem_limit_bytes=VMEM_LIMIT),
        name="merge",
    )(o, sza, sga, gp, x, w_ao, w_o, fg)


def _layer(x, left16, tabs, past_t, wts, fg, prev, *, layer, depth, past, tm, qb, kc, kv_t):
    g, w_in, w_mix, pscale, w_po, w_ao, w_o = wts
    B, T, _ = x.shape
    gp, q, kb, k_all, v_all, qi, slab, kit_all, sza, sga, pstate_all = _proj(
        x, left16, tabs, g, w_in, w_mix, pscale, w_po, prev, layer=layer, depth=depth, past=past, tm=tm, kv_t=kv_t)
    o = _attend(q, qi, slab, kb, v_all, past_t, layer=layer, qb=qb, kc=kc, kv_t=kv_t)
    flat = lambda a: a.reshape(B * T, a.shape[-1])
    y = _merge(flat(o), flat(sza), flat(sga), flat(gp), flat(x), w_ao, w_o, fg,
               final=layer == depth - 1, tm=min(256, B * T))
    return y.reshape(B, T, D_MODEL), (k_all, v_all, kit_all, pstate_all)


def kernel(x_prompt, x_sample, cache_k, cache_v, cache_kidx, state_pool, norm_g, w_in, w_pool_mix, pool_scale,
           w_pool_out, w_attn_out, w_o, final_norm_g):
    Bp, Tp, _ = x_prompt.shape
    Bs, Ts, _ = x_sample.shape
    depth = w_in.shape[0]
    past = cache_k.shape[2]
    tabs_p = _rope_tables(jnp.arange(Tp, dtype=jnp.int32))
    tabs_s = _rope_tables(past + jnp.arange(Ts, dtype=jnp.int32))
    left_p = jnp.zeros((Bp, HALO, POOL_WIDTH), F32)
    fg = final_norm_g.reshape(1, D_MODEL)
    tm_p = min(256, Tp)
    past_t = (jnp.transpose(cache_k, (0, 1, 3, 4, 2)).reshape(depth, Bs, ATTN_WIDTH, past),
              jnp.transpose(cache_v, (0, 1, 3, 4, 2)).reshape(depth, Bs, ATTN_WIDTH, past),
              jnp.transpose(cache_kidx, (0, 1, 3, 2)))

    xp, xs = x_prompt, x_sample
    prev_p = prev_s = None
    for l in range(depth):
        wts = (norm_g[l].reshape(1, D_MODEL), _prep_w_in(w_in[l]), w_pool_mix[l].astype(BF16),
               pool_scale[l].reshape(1, POOL_WIDTH), w_pool_out[l].astype(BF16), w_attn_out[l].astype(BF16),
               w_o[l].astype(BF16))
        xp, prev_p = _layer(xp, left_p, tabs_p, None, wts, fg, prev_p, layer=l, depth=depth, past=0,
                            tm=tm_p, qb=tm_p, kc=tm_p, kv_t=True)
        left_s = jnp.pad(state_pool[l], ((0, 0), (HALO - POOL_STATE, 0), (0, 0)))
        xs, prev_s = _layer(xs, left_s, tabs_s, past_t, wts, fg, prev_s, layer=l, depth=depth, past=past,
                            tm=Ts, qb=Ts, kc=256, kv_t=False)
    kp, vp, kitp, psp = prev_p
    ks, vs, kits, pss = prev_s
    heads_t = lambda a: jnp.transpose(a.reshape(depth, Bp, N_HEADS, HEAD_DIM, Tp), (0, 1, 4, 2, 3))
    heads = lambda a: a.reshape(depth, Bs, Ts, N_HEADS, HEAD_DIM)
    return (xp, xs,
            heads_t(kp), heads_t(vp), jnp.transpose(kitp, (0, 1, 3, 2)), psp[:, :, HALO - POOL_STATE:, :],
            heads(ks), heads(vs), jnp.transpose(kits, (0, 1, 3, 2)), pss[:, :, HALO - POOL_STATE:, :])
```

```python
import functools

import jax
import jax.numpy as jnp
import numpy as np
from jax import lax
from jax.experimental import pallas as pl
from jax.experimental.pallas import tpu as pltpu

F32 = jnp.float32
BF16 = jnp.bfloat16
I32 = jnp.int32

D_MODEL = 1024
CHUNK = 64
POOL_WIDTH = 512
POOL_WINDOWS = (2, 4, 8, 16)
POOL_GROUP_WIDTH = 128
POOL_STATE = 15
N_HEADS = 8
HEAD_DIM = 64
ATTN_WIDTH = N_HEADS * HEAD_DIM
IDX_HEADS = 8
IDX_DIM = 32
IDX_WIDTH = IDX_HEADS * IDX_DIM
TOPK_MAX = 256
ROPE_THETA = 10000.0
EPS = 1e-6

LANES = 128
PACK = 16
HALO = 16
INT_MIN = -2147483648
MIN_NORMAL_TOP16 = 0x0080
NEG_BIG = -1e30
LOG2E = 1.4426950408889634
Q_SCALE = HEAD_DIM ** -0.5 * LOG2E
VT_ROWS = HEAD_DIM + PACK
VMEM_LIMIT = 56 * 1024 * 1024

SEG_U, SEG_ZP, SEG_Q, SEG_K, SEG_V = 0, 512, 1024, 1536, 2048
SEG_QI, SEG_ZA, SEG_GP, SEG_GA, SEG_MISC = 2560, 2816, 3328, 4352, 5376
PROJ_COLS = 5504
_ORIG = np.cumsum([0, 512, 512, 512, 512, 512, 256, 32, 8, 512, 1024, 1024])


def _prep_w_in(w):
    o = _ORIG
    main = [w[:, o[0]:o[6]], w[:, o[8]:o[11]]]
    misc = jnp.concatenate([w[:, o[6]:o[8]], jnp.zeros((w.shape[0], LANES - IDX_DIM - IDX_HEADS), w.dtype)], axis=1)
    return jnp.concatenate(main + [misc], axis=1).astype(BF16)


def _rope_tables(pos):
    lane = jnp.arange(LANES)

    def tabs(d, passthrough_from=None):
        half = d // 2
        inv = ROPE_THETA ** (-jnp.arange(0, d, 2, dtype=F32) / d)
        ang = pos.astype(F32)[:, None] * inv[None, :]
        cos = jnp.cos(ang)[:, lane % half]
        sin = jnp.sin(ang)[:, lane % half]
        lo = (lane % d) < half
        sin_lo = jnp.where(lo[None, :], -sin, 0.0)
        sin_hi = jnp.where(lo[None, :], 0.0, sin)
        if passthrough_from is not None:
            keep = (lane < passthrough_from)[None, :]
            cos = jnp.where(keep, cos, 1.0)
            sin_lo = jnp.where(keep, sin_lo, 0.0)
            sin_hi = jnp.where(keep, sin_hi, 0.0)
        return [cos, sin_lo, sin_hi]

    return jnp.stack(tabs(HEAD_DIM) + tabs(IDX_DIM) + tabs(IDX_DIM, IDX_DIM)).astype(F32)


def _rope(x, cos, sin_lo, sin_hi, half):
    return x * cos + pltpu.roll(x, LANES - half, 1) * sin_lo + pltpu.roll(x, half, 1) * sin_hi


def _silu(x):
    return x * jax.nn.sigmoid(x)


def _pad_rows(a, rows):
    if a.shape[0] == rows:
        return a
    return jnp.concatenate([a, jnp.zeros((rows - a.shape[0],) + a.shape[1:], a.dtype)], axis=0)


def _transpose(a):
    r = a.shape[0]
    rp = -(-r // LANES) * LANES
    at = _pad_rows(a, rp).T
    return at if rp == r else at[:, :r]


def _proj_kernel(*refs, tm, past, kv_t, n_alias):
    (x_ref, left_ref, tab_ref, g_ref, win_ref, wmix_ref, pscale_ref, wpo_ref) = refs[:8]
    (gp_ref, q_ref, kb_ref, k_ref, v_ref, qi_ref, slab_ref, kit_ref, sza_ref, sga_ref, pstate_ref,
     h_scr, u_scr, pg_scr) = refs[8 + n_alias:]
    t = pl.program_id(1)
    x = x_ref[0]
    r = lax.rsqrt(jnp.mean(x * x, axis=-1, keepdims=True) + EPS)
    h_scr[...] = (x * r * g_ref[...]).astype(BF16)

    def seg(a, n):
        return jnp.dot(h_scr[...], win_ref[:, a:a + n], preferred_element_type=F32)

    @pl.when(t == 0)
    def _():
        u_scr[0:HALO, :] = left_ref[0]

    u_scr[HALO:HALO + tm, :] = seg(SEG_U, POOL_WIDTH)
    pos = past + t * tm + lax.broadcasted_iota(I32, (tm, 1), 0)
    zp = seg(SEG_ZP, POOL_WIDTH)
    for gi, w in enumerate(POOL_WINDOWS):
        cols = slice(gi * POOL_GROUP_WIDTH, (gi + 1) * POOL_GROUP_WIDTH)
        u_g = u_scr[HALO:HALO + tm, cols]
        s = u_g
        for j in range(1, w):
            s = s + u_scr[HALO - j:HALO - j + tm, cols]
        cnt = jnp.minimum(pos + 1, w).astype(F32)
        pooled = s / cnt - u_g
        mixed = jnp.dot(pooled.astype(BF16), wmix_ref[gi], preferred_element_type=F32) * pscale_ref[:, cols]
        pg_scr[:, cols] = (mixed * _silu(zp[:, cols])).astype(BF16)
    br_pool = jnp.dot(pg_scr[...], wpo_ref[...], preferred_element_type=F32)
    gp_ref[0] = jax.nn.sigmoid(seg(SEG_GP, D_MODEL)) * br_pool
    pstate_ref[0, 0] = u_scr[tm:tm + HALO, :]
    u_scr[0:HALO, :] = u_scr[tm:tm + HALO, :]

    c64, lo64, hi64 = tab_ref[0], tab_ref[1], tab_ref[2]
    c32, lo32, hi32 = tab_ref[3], tab_ref[4], tab_ref[5]
    cm, lom, him = tab_ref[6], tab_ref[7], tab_ref[8]
    qs = seg(SEG_Q, ATTN_WIDTH)
    ks = seg(SEG_K, ATTN_WIDTH)
    for c in range(ATTN_WIDTH // LANES):
        cols = slice(c * LANES, (c + 1) * LANES)
        q_ref[0, :, cols] = (_rope(qs[:, cols], c64, lo64, hi64, HEAD_DIM // 2) * Q_SCALE).astype(BF16)
        kr = _rope(ks[:, cols], c64, lo64, hi64, HEAD_DIM // 2)
        kb_ref[0, :, cols] = kr.astype(BF16)
        if kv_t:
            k_ref[0, 0, cols, :] = _transpose(kr)
        else:
            k_ref[0, 0, :, cols] = kr
    v = seg(SEG_V, ATTN_WIDTH)
    v_ref[0, 0] = _transpose(v) if kv_t else v
    qis = seg(SEG_QI, IDX_WIDTH)
    for c in range(IDX_WIDTH // LANES):
        cols = slice(c * LANES, (c + 1) * LANES)
        qi_ref[0, :, cols] = _rope(qis[:, cols], c32, lo32, hi32, IDX_DIM // 2).astype(BF16)
    slab = _rope(seg(SEG_MISC, LANES), cm, lom, him, IDX_DIM // 2)
    slab_ref[0] = slab
    kit_ref[0, 0] = _transpose(slab)[0:IDX_DIM, :]
    sza_ref[0] = _silu(seg(SEG_ZA, ATTN_WIDTH))
    sga_ref[0] = jax.nn.sigmoid(seg(SEG_GA, D_MODEL))


def _proj(x, left16, tabs, g, w_in, w_mix, pscale, w_po, prev, *, layer, depth, past, tm, kv_t):
    B, T, _ = x.shape
    nt = T // tm
    const = dict(pipeline_mode=pl.Buffered(1))
    row = lambda n: pl.BlockSpec((1, tm, n), lambda b, t: (b, t, 0))
    if kv_t:
        kv_shape = (depth, B, ATTN_WIDTH, T)
        kv_spec = pl.BlockSpec((1, 1, ATTN_WIDTH, tm), lambda b, t: (layer, b, 0, t))
    else:
        kv_shape = (depth, B, T, ATTN_WIDTH)
        kv_spec = pl.BlockSpec((1, 1, tm, ATTN_WIDTH), lambda b, t: (layer, b, t, 0))
    out_shape = [
        jax.ShapeDtypeStruct((B, T, D_MODEL), F32),
        jax.ShapeDtypeStruct((B, T, ATTN_WIDTH), BF16),
        jax.ShapeDtypeStruct((B, T, ATTN_WIDTH), BF16),
        jax.ShapeDtypeStruct(kv_shape, F32),
        jax.ShapeDtypeStruct(kv_shape, F32),
        jax.ShapeDtypeStruct((B, T, IDX_WIDTH), BF16),
        jax.ShapeDtypeStruct((B, T, LANES), F32),
        jax.ShapeDtypeStruct((depth, B, IDX_DIM, T), F32),
        jax.ShapeDtypeStruct((B, T, ATTN_WIDTH), F32),
        jax.ShapeDtypeStruct((B, T, D_MODEL), F32),
        jax.ShapeDtypeStruct((depth, B, HALO, POOL_WIDTH), F32),
    ]
    out_specs = [row(D_MODEL), row(ATTN_WIDTH), row(ATTN_WIDTH), kv_spec, kv_spec, row(IDX_WIDTH), row(LANES),
                 pl.BlockSpec((1, 1, IDX_DIM, tm), lambda b, t: (layer, b, 0, t)),
                 row(ATTN_WIDTH), row(D_MODEL),
                 pl.BlockSpec((1, 1, HALO, POOL_WIDTH), lambda b, t: (layer, b, 0, 0))]
    in_specs = [
        row(D_MODEL),
        pl.BlockSpec((1, HALO, POOL_WIDTH), lambda b, t: (b, 0, 0)),
        pl.BlockSpec((9, tm, LANES), lambda b, t: (0, t, 0)),
        pl.BlockSpec((1, D_MODEL), lambda b, t: (0, 0), **const),
        pl.BlockSpec((D_MODEL, PROJ_COLS), lambda b, t: (0, 0), **const),
        pl.BlockSpec((4, POOL_GROUP_WIDTH, POOL_GROUP_WIDTH), lambda b, t: (0, 0, 0), **const),
        pl.BlockSpec((1, POOL_WIDTH), lambda b, t: (0, 0), **const),
        pl.BlockSpec((POOL_WIDTH, D_MODEL), lambda b, t: (0, 0), **const),
    ]
    args = [x, left16, tabs, g, w_in, w_mix, pscale, w_po]
    aliases = {}
    if prev is not None:
        for arr, out_idx in zip(prev, (3, 4, 7, 10)):
            aliases[len(args)] = out_idx
            in_specs.append(pl.BlockSpec(memory_space=pl.ANY))
            args.append(arr)
    return pl.pallas_call(
        functools.partial(_proj_kernel, tm=tm, past=past, kv_t=kv_t, n_alias=len(aliases)),
        grid=(B, nt),
        in_specs=in_specs,
        out_specs=out_specs,
        out_shape=out_shape,
        input_output_aliases=aliases,
        scratch_shapes=[pltpu.VMEM((tm, D_MODEL), BF16),
                        pltpu.VMEM((tm + HALO, POOL_WIDTH), F32),
                        pltpu.VMEM((tm, POOL_WIDTH), BF16)],
        compiler_params=pltpu.CompilerParams(dimension_semantics=("arbitrary", "arbitrary"),
                                             vmem_limit_bytes=VMEM_LIMIT),
        name="proj",
    )(*args)


def _attend_kernel(*refs, qb, ql, kc, n_past, past, total, ksel, has_past, kv_t, final):
    if has_past:
        (q_ref, qi_ref, slab_ref, k_ref, v_ref, pk_ref, pv_ref, pki_ref) = refs[:8]
        rest = refs[8:]
    else:
        (q_ref, qi_ref, slab_ref, k_ref, v_ref) = refs[:5]
        rest = refs[5:]
    (rep_ref, rept_ref, tri_ref, sza_ref, sga_ref, gp_ref, x_ref, wao_ref, wo_ref, fg_ref) = rest[:10]
    rest = rest[10:]
    (y_ref, kbuf, vtbuf, kibuf, key_scr, rk_scr, bias_scr, ot_scr, on_scr, qm_scr, qim_scr, m_scr, a_scr, s_scr,
     p_scr, sacc_scr) = rest
    s = pl.program_id(1)
    contract_last = (((1,), (1,)), ((), ()))

    def store_values(j, vt):
        for h in range(N_HEADS):
            vtbuf[j, h, 0:HEAD_DIM, :] = vt[h * HEAD_DIM:(h + 1) * HEAD_DIM, :].astype(BF16)
            vtbuf[j, h, HEAD_DIM:VT_ROWS, :] = jnp.ones((PACK, kc), BF16)

    if has_past:
        @pl.when(s < n_past)
        def _():
            kbuf[s] = pk_ref[0, 0].T.astype(BF16)
            store_values(s, pv_ref[0, 0])
            ki_t = jnp.dot(rept_ref[...], pki_ref[0, 0].astype(BF16), preferred_element_type=F32)
            kibuf[s] = ki_t.T.astype(BF16)

    @pl.when(s >= n_past)
    def _():
        i = s - n_past
        c = n_past + i
        slab = _pad_rows(slab_ref[0], kc)
        kbuf[c] = _pad_rows(k_ref[0], kc)
        store_values(c, v_ref[0, 0] if kv_t else _pad_rows(v_ref[0, 0], kc).T)
        kibuf[c] = jnp.dot(slab.astype(BF16), rep_ref[...], preferred_element_type=F32).astype(BF16)

        w_t = _pad_rows(slab_ref[0], ql).T[IDX_DIM:IDX_DIM + IDX_HEADS, :] * (IDX_DIM ** -0.5 * IDX_HEADS ** -0.5)
        qi = _pad_rows(qi_ref[0], ql)
        q = _pad_rows(q_ref[0], ql)
        lane = lax.broadcasted_iota(I32, (ql, LANES), 1)
        heads_per_tile = LANES // IDX_DIM
        zero = jnp.zeros((), BF16)
        for h in range(IDX_HEADS):
            tile = h // heads_per_tile
            qim_scr[h] = jnp.where(lane // IDX_DIM == h % heads_per_tile, qi[:, tile * LANES:(tile + 1) * LANES], zero)
        for h in range(N_HEADS):
            pair = h // 2
            qm_scr[h] = jnp.where(lane // HEAD_DIM == h % 2, q[:, pair * LANES:(pair + 1) * LANES], zero)

        def score_keys(j):
            ki = kibuf[j]
            for h in range(IDX_HEADS):
                sh = lax.dot_general(ki, qim_scr[h], contract_last, preferred_element_type=F32)
                term = w_t[h:h + 1, :] * jnp.maximum(sh, 0.0)
                sacc_scr[...] = term if h == 0 else sacc_scr[...] + term
            acc = sacc_scr[...]
            bits = pltpu.bitcast(acc, I32)
            key = bits ^ ((bits >> 31) & 0x7FFFFFFF)
            return jnp.where(acc == 0.0, 0, key)

        def store_keys(j, key):
            key_scr[j] = key
            score_bits = key ^ ((key >> 31) & 0x7FFFFFFF)
            rk_scr[j] = pltpu.bitcast(score_bits & -65536, F32).astype(BF16)

        def full_chunk(j, carry):
            store_keys(j, score_keys(j))
            return carry

        lax.fori_loop(0, c, full_chunk, 0)
        kpos = c * kc + lax.broadcasted_iota(I32, (kc, ql), 0)
        qpos = past + i * qb + lax.broadcasted_iota(I32, (kc, ql), 1)
        adm = (kpos // CHUNK <= qpos // CHUNK) & (kpos < total)
        store_keys(c, jnp.where(adm, score_keys(c), INT_MIN))

        one, nil = jnp.ones((), BF16), jnp.zeros((), BF16)

        def count_packed(cand, strict):
            cb = jnp.broadcast_to(cand, (PACK, ql)).astype(BF16)[None]

            def body(j, acc):
                x = rk_scr[j].reshape(kc // PACK, PACK, ql)
                hit = (x > cb) if strict else (x >= cb)
                parts = list(jnp.where(hit, one, nil))
                while len(parts) > 1:
                    parts = [a + b for a, b in zip(parts[::2], parts[1::2])]
                return acc + parts[0].astype(F32)
            acc = lax.fori_loop(0, c + 1, body, jnp.zeros((PACK, ql), F32))
            return acc.sum(axis=0, keepdims=True)

        def top16_as_float(k16):
            pat = k16 ^ ((k16 >> 31) & 0x7FFF)
            pat = jnp.where((pat > 0) & (pat < MIN_NORMAL_TOP16), MIN_NORMAL_TOP16, pat)
            return pltpu.bitcast(pat << 16, F32)

        def digit_search(nbits, to_float, want):
            def step(p, pre):
                cand = pre | (jnp.int32(1) << (nbits - 1 - p))
                return jnp.where(count_packed(to_float(cand), False) >= want, cand, pre)
            return lax.fori_loop(0, nbits, step, jnp.zeros((1, ql), I32))

        def load_digit(shift, upper):
            def body(j, carry):
                kj = key_scr[j]
                d = jnp.where((kj >> (shift + 8)) == upper, (kj >> shift) & 0xFF, -1)
                rk_scr[j] = d.astype(F32).astype(BF16)
                return carry
            lax.fori_loop(0, c + 1, body, 0)

        half = 1 << 15
        hi = digit_search(16, lambda u: top16_as_float(u - half), ksel) - half
        want = ksel - count_packed(top16_as_float(hi), True)
        load_digit(8, hi)
        mid = digit_search(8, lambda d: d.astype(F32), want)
        want = want - count_packed(mid.astype(F32), True)
        upper = (hi << 8) | mid
        load_digit(0, upper)
        low = digit_search(8, lambda d: d.astype(F32), want)
        thr = (upper << 8) | low
        above = count_packed(low.astype(F32), True)
        ties = count_packed(low.astype(F32), False) - above
        need = want - above
        live = thr != INT_MIN
        real = lax.broadcasted_iota(I32, (1, ql), 1) < qb
        surplus = jnp.max(jnp.where(live & real & (ties > need), 1.0, 0.0)) > 0.0

        @pl.when(jnp.logical_not(surplus))
        def _():
            def select_chunk(j, carry):
                kj = key_scr[j]
                bias_scr[j] = jnp.where((kj > thr) | ((kj == thr) & live), 0.0, NEG_BIG)
                return carry
            lax.fori_loop(0, c + 1, select_chunk, 0)

        @pl.when(surplus)
        def _():
            def select_chunk(j, seen):
                kj = key_scr[j]
                eq = kj == thr
                eqf = jnp.where(eq, 1.0, 0.0)
                rank = seen + jnp.dot(tri_ref[...], eqf.astype(BF16), preferred_element_type=F32)
                sel = (kj > thr) | (eq & (rank < need) & live)
                bias_scr[j] = jnp.where(sel, 0.0, NEG_BIG)
                return seen + eqf.reshape(kc // 8, 8, ql).sum(axis=0).sum(axis=0, keepdims=True)
            lax.fori_loop(0, c + 1, select_chunk, jnp.zeros((1, ql), F32))

        m_scr[...] = jnp.full((N_HEADS, ql), NEG_BIG, F32)
        ot_scr[...] = jnp.zeros((N_HEADS, VT_ROWS, ql), F32)

        def attend_chunk(j, carry):
            for h in range(N_HEADS):
                pair = h // 2
                kj = kbuf[j, :, pair * LANES:(pair + 1) * LANES]
                s_scr[h] = lax.dot_general(kj, qm_scr[h], contract_last, preferred_element_type=F32) + bias_scr[j]
            for h in range(N_HEADS):
                mx = s_scr[h].reshape(kc // 8, 8, ql).max(axis=0).max(axis=0, keepdims=True)
                m = m_scr[h:h + 1, :]
                m_new = jnp.maximum(m, mx)
                a_scr[h:h + 1, :] = jnp.exp2(m - m_new)
                m_scr[h:h + 1, :] = m_new
            for h in range(N_HEADS):
                p_scr[h] = jnp.exp2(s_scr[h] - m_scr[h:h + 1, :]).astype(BF16)
            for h in range(N_HEADS):
                ot_scr[h] = a_scr[h:h + 1, :] * ot_scr[h] + jnp.dot(vtbuf[j, h], p_scr[h],
                                                                    preferred_element_type=F32)
            return carry

        lax.fori_loop(0, c + 1, attend_chunk, 0)
        for h in range(N_HEADS):
            rows = slice(h * HEAD_DIM, (h + 1) * HEAD_DIM)
            on_scr[rows, :] = ot_scr[h, 0:HEAD_DIM, :] / ot_scr[h, HEAD_DIM:HEAD_DIM + 1, :]

        a = (on_scr[...].T[0:qb, :] * sza_ref[0]).astype(BF16)
        br_attn = jnp.dot(a, wao_ref[...], preferred_element_type=F32)
        merged = gp_ref[0] + sga_ref[0] * br_attn
        y = x_ref[0] + jnp.dot(merged.astype(BF16), wo_ref[...], preferred_element_type=F32)
        if final:
            r = lax.rsqrt(jnp.mean(y * y, axis=-1, keepdims=True) + EPS)
            y = y * r * fg_ref[...]
        y_ref[0] = y


def _attend(q, qi, slab, kb, v_all, past_t, sza, sga, gp, x, w_ao, w_o, fg, *, layer, qb, kc, kv_t, final):
    B, T, _ = q.shape
    has_past = past_t is not None
    past = past_t[0].shape[3] if has_past else 0
    total = past + T
    assert past % kc == 0 and T % qb == 0 and (qb == kc or T == qb)
    n_past = past // kc
    nb = T // qb
    nc = n_past + (T + kc - 1) // kc
    ql = max(qb, LANES)
    ksel = min(TOPK_MAX, total // 4)
    lane = jnp.arange(LANES)
    rep = (lane[:, None] == (lane[None, :] % IDX_DIM)) & (lane[:, None] < IDX_DIM)
    tri = jnp.arange(kc)[:, None] > jnp.arange(kc)[None, :]

    blk = lambda s: jnp.maximum(s - n_past, 0)
    cur = lambda n: pl.BlockSpec((1, qb, n), lambda b, s: (b, blk(s), 0))
    if kv_t:
        v_spec = pl.BlockSpec((1, 1, ATTN_WIDTH, qb), lambda b, s: (layer, b, 0, blk(s)))
    else:
        v_spec = pl.BlockSpec((1, 1, qb, ATTN_WIDTH), lambda b, s: (layer, b, blk(s), 0))
    in_specs = [cur(ATTN_WIDTH), cur(IDX_WIDTH), cur(LANES), cur(ATTN_WIDTH), v_spec]
    args = [q, qi, slab, kb, v_all]
    if has_past:
        old = lambda n: pl.BlockSpec((1, 1, n, kc), lambda b, s: (layer, b, 0, jnp.minimum(s, n_past - 1)))
        in_specs += [old(ATTN_WIDTH), old(ATTN_WIDTH), old(IDX_DIM)]
        args += list(past_t)
    in_specs += [pl.BlockSpec((LANES, LANES), lambda b, s: (0, 0)),
                 pl.BlockSpec((LANES, IDX_DIM), lambda b, s: (0, 0)),
                 pl.BlockSpec((kc, kc), lambda b, s: (0, 0))]
    args += [rep.astype(BF16), rep[:IDX_DIM, :].T.astype(BF16), tri.astype(BF16)]
    const = dict(pipeline_mode=pl.Buffered(1))
    in_specs += [cur(ATTN_WIDTH), cur(D_MODEL), cur(D_MODEL), cur(D_MODEL),
                 pl.BlockSpec((ATTN_WIDTH, D_MODEL), lambda b, s: (0, 0), **const),
                 pl.BlockSpec((D_MODEL, D_MODEL), lambda b, s: (0, 0), **const),
                 pl.BlockSpec((1, D_MODEL), lambda b, s: (0, 0), **const)]
    args += [sza, sga, gp, x, w_ao, w_o, fg]
    return pl.pallas_call(
        functools.partial(_attend_kernel, qb=qb, ql=ql, kc=kc, n_past=n_past, past=past, total=total,
                          ksel=ksel, has_past=has_past, kv_t=kv_t, final=final),
        grid=(B, n_past + nb),
        in_specs=in_specs,
        out_specs=cur(D_MODEL),
        out_shape=jax.ShapeDtypeStruct((B, T, D_MODEL), F32),
        scratch_shapes=[pltpu.VMEM((nc, kc, ATTN_WIDTH), BF16),
                        pltpu.VMEM((nc, N_HEADS, VT_ROWS, kc), BF16),
                        pltpu.VMEM((nc, kc, LANES), BF16),
                        pltpu.VMEM((nc, kc, ql), I32),
                        pltpu.VMEM((nc, kc, ql), BF16),
                        pltpu.VMEM((nc, kc, ql), F32),
                        pltpu.VMEM((N_HEADS, VT_ROWS, ql), F32),
                        pltpu.VMEM((ATTN_WIDTH, ql), F32),
                        pltpu.VMEM((N_HEADS, ql, LANES), BF16),
                        pltpu.VMEM((IDX_HEADS, ql, LANES), BF16),
                        pltpu.VMEM((N_HEADS, ql), F32),
                        pltpu.VMEM((N_HEADS, ql), F32),
                        pltpu.VMEM((N_HEADS, kc, ql), F32),
                        pltpu.VMEM((N_HEADS, kc, ql), BF16),
                        pltpu.VMEM((kc, ql), F32)],
        compiler_params=pltpu.CompilerParams(dimension_semantics=("arbitrary", "arbitrary"),
                                             vmem_limit_bytes=VMEM_LIMIT),
        name="attend",
    )(*args)


def _layer(x, left16, tabs, past_t, wts, fg, prev, *, layer, depth, past, tm, qb, kc, kv_t):
    g, w_in, w_mix, pscale, w_po, w_ao, w_o = wts
    B, T, _ = x.shape
    gp, q, kb, k_all, v_all, qi, slab, kit_all, sza, sga, pstate_all = _proj(
        x, left16, tabs, g, w_in, w_mix, pscale, w_po, prev, layer=layer, depth=depth, past=past, tm=tm, kv_t=kv_t)
    y = _attend(q, qi, slab, kb, v_all, past_t, sza, sga, gp, x, w_ao, w_o, fg,
                layer=layer, qb=qb, kc=kc, kv_t=kv_t, final=layer == depth - 1)
    return y, (k_all, v_all, kit_all, pstate_all)


def kernel(x_prompt, x_sample, cache_k, cache_v, cache_kidx, state_pool, norm_g, w_in, w_pool_mix, pool_scale,
           w_pool_out, w_attn_out, w_o, final_norm_g):
    Bp, Tp, _ = x_prompt.shape
    Bs, Ts, _ = x_sample.shape
    depth = w_in.shape[0]
    past = cache_k.shape[2]
    tabs_p = _rope_tables(jnp.arange(Tp, dtype=jnp.int32))
    tabs_s = _rope_tables(past + jnp.arange(Ts, dtype=jnp.int32))
    left_p = jnp.zeros((Bp, HALO, POOL_WIDTH), F32)
    fg = final_norm_g.reshape(1, D_MODEL)
    tm_p = min(256, Tp)
    past_t = (jnp.transpose(cache_k, (0, 1, 3, 4, 2)).reshape(depth, Bs, ATTN_WIDTH, past),
              jnp.transpose(cache_v, (0, 1, 3, 4, 2)).reshape(depth, Bs, ATTN_WIDTH, past),
              jnp.transpose(cache_kidx, (0, 1, 3, 2)))

    xp, xs = x_prompt, x_sample
    prev_p = prev_s = None
    for l in range(depth):
        wts = (norm_g[l].reshape(1, D_MODEL), _prep_w_in(w_in[l]), w_pool_mix[l].astype(BF16),
               pool_scale[l].reshape(1, POOL_WIDTH), w_pool_out[l].astype(BF16), w_attn_out[l].astype(BF16),
               w_o[l].astype(BF16))
        xp, prev_p = _layer(xp, left_p, tabs_p, None, wts, fg, prev_p, layer=l, depth=depth, past=0,
                            tm=tm_p, qb=tm_p, kc=tm_p, kv_t=True)
        left_s = jnp.pad(state_pool[l], ((0, 0), (HALO - POOL_STATE, 0), (0, 0)))
        xs, prev_s = _layer(xs, left_s, tabs_s, past_t, wts, fg, prev_s, layer=l, depth=depth, past=past,
                            tm=Ts, qb=Ts, kc=256, kv_t=False)
    kp, vp, kitp, psp = prev_p
    ks, vs, kits, pss = prev_s
    heads_t = lambda a: jnp.transpose(a.reshape(depth, Bp, N_HEADS, HEAD_DIM, Tp), (0, 1, 4, 2, 3))
    heads = lambda a: a.reshape(depth, Bs, Ts, N_HEADS, HEAD_DIM)
    return (xp, xs,
            heads_t(kp), heads_t(vp), jnp.transpose(kitp, (0, 1, 3, 2)), psp[:, :, HALO - POOL_STATE:, :],
            heads(ks), heads(vs), jnp.transpose(kits, (0, 1, 3, 2)), pss[:, :, HALO - POOL_STATE:, :])
```

```python
import functools

import jax
import jax.numpy as jnp
import numpy as np
from jax import lax
from jax.experimental import pallas as pl
from jax.experimental.pallas import tpu as pltpu

F32 = jnp.float32
BF16 = jnp.bfloat16
I32 = jnp.int32

D_MODEL = 1024
CHUNK = 64
POOL_WIDTH = 512
POOL_WINDOWS = (2, 4, 8, 16)
POOL_GROUP_WIDTH = 128
POOL_STATE = 15
N_HEADS = 8
HEAD_DIM = 64
ATTN_WIDTH = N_HEADS * HEAD_DIM
IDX_HEADS = 8
IDX_DIM = 32
IDX_WIDTH = IDX_HEADS * IDX_DIM
TOPK_MAX = 256
ROPE_THETA = 10000.0
EPS = 1e-6

LANES = 128
PACK = 16
HALO = 16
INT_MIN = -2147483648
MIN_NORMAL_TOP16 = 0x0080
NEG_BIG = -1e30
LOG2E = 1.4426950408889634
Q_SCALE = HEAD_DIM ** -0.5 * LOG2E
VT_ROWS = HEAD_DIM + PACK
VMEM_LIMIT = 56 * 1024 * 1024

SEG_U, SEG_ZP, SEG_Q, SEG_K, SEG_V = 0, 512, 1024, 1536, 2048
SEG_QI, SEG_ZA, SEG_GP, SEG_GA, SEG_MISC = 2560, 2816, 3328, 4352, 5376
PROJ_COLS = 5504
_ORIG = np.cumsum([0, 512, 512, 512, 512, 512, 256, 32, 8, 512, 1024, 1024])


def _prep_w_in(w):
    o = _ORIG
    main = [w[:, o[0]:o[6]], w[:, o[8]:o[11]]]
    misc = jnp.concatenate([w[:, o[6]:o[8]], jnp.zeros((w.shape[0], LANES - IDX_DIM - IDX_HEADS), w.dtype)], axis=1)
    return jnp.concatenate(main + [misc], axis=1).astype(BF16)


def _rope_tables(pos):
    lane = jnp.arange(LANES)

    def tabs(d, passthrough_from=None):
        half = d // 2
        inv = ROPE_THETA ** (-jnp.arange(0, d, 2, dtype=F32) / d)
        ang = pos.astype(F32)[:, None] * inv[None, :]
        cos = jnp.cos(ang)[:, lane % half]
        sin = jnp.sin(ang)[:, lane % half]
        lo = (lane % d) < half
        sin_lo = jnp.where(lo[None, :], -sin, 0.0)
        sin_hi = jnp.where(lo[None, :], 0.0, sin)
        if passthrough_from is not None:
            keep = (lane < passthrough_from)[None, :]
            cos = jnp.where(keep, cos, 1.0)
            sin_lo = jnp.where(keep, sin_lo, 0.0)
            sin_hi = jnp.where(keep, sin_hi, 0.0)
        return [cos, sin_lo, sin_hi]

    return jnp.stack(tabs(HEAD_DIM) + tabs(IDX_DIM) + tabs(IDX_DIM, IDX_DIM)).astype(F32)


def _rope(x, cos, sin_lo, sin_hi, half):
    return x * cos + pltpu.roll(x, LANES - half, 1) * sin_lo + pltpu.roll(x, half, 1) * sin_hi


def _silu(x):
    return x * jax.nn.sigmoid(x)


def _pad_rows(a, rows):
    if a.shape[0] == rows:
        return a
    return jnp.concatenate([a, jnp.zeros((rows - a.shape[0],) + a.shape[1:], a.dtype)], axis=0)


def _transpose(a):
    r = a.shape[0]
    rp = -(-r // LANES) * LANES
    at = _pad_rows(a, rp).T
    return at if rp == r else at[:, :r]


def _proj_kernel(*refs, tm, past, kv_t, n_alias):
    (x_ref, left_ref, tab_ref, g_ref, win_ref, wmix_ref, pscale_ref, wpo_ref) = refs[:8]
    (gp_ref, q_ref, kb_ref, k_ref, v_ref, qi_ref, slab_ref, kit_ref, sza_ref, sga_ref, pstate_ref,
     h_scr, u_scr, pg_scr) = refs[8 + n_alias:]
    t = pl.program_id(1)
    x = x_ref[0]
    r = lax.rsqrt(jnp.mean(x * x, axis=-1, keepdims=True) + EPS)
    h_scr[...] = (x * r * g_ref[...]).astype(BF16)

    def seg(a, n):
        return jnp.dot(h_scr[...], win_ref[:, a:a + n], preferred_element_type=F32)

    @pl.when(t == 0)
    def _():
        u_scr[0:HALO, :] = left_ref[0]

    u_scr[HALO:HALO + tm, :] = seg(SEG_U, POOL_WIDTH)
    pos = past + t * tm + lax.broadcasted_iota(I32, (tm, 1), 0)
    zp = seg(SEG_ZP, POOL_WIDTH)
    for gi, w in enumerate(POOL_WINDOWS):
        cols = slice(gi * POOL_GROUP_WIDTH, (gi + 1) * POOL_GROUP_WIDTH)
        u_g = u_scr[HALO:HALO + tm, cols]
        s = u_g
        for j in range(1, w):
            s = s + u_scr[HALO - j:HALO - j + tm, cols]
        cnt = jnp.minimum(pos + 1, w).astype(F32)
        pooled = s / cnt - u_g
        mixed = jnp.dot(pooled.astype(BF16), wmix_ref[gi], preferred_element_type=F32) * pscale_ref[:, cols]
        pg_scr[:, cols] = (mixed * _silu(zp[:, cols])).astype(BF16)
    br_pool = jnp.dot(pg_scr[...], wpo_ref[...], preferred_element_type=F32)
    gp_ref[0] = jax.nn.sigmoid(seg(SEG_GP, D_MODEL)) * br_pool
    pstate_ref[0, 0] = u_scr[tm:tm + HALO, :]
    u_scr[0:HALO, :] = u_scr[tm:tm + HALO, :]

    c64, lo64, hi64 = tab_ref[0], tab_ref[1], tab_ref[2]
    c32, lo32, hi32 = tab_ref[3], tab_ref[4], tab_ref[5]
    cm, lom, him = tab_ref[6], tab_ref[7], tab_ref[8]
    qs = seg(SEG_Q, ATTN_WIDTH)
    ks = seg(SEG_K, ATTN_WIDTH)
    for c in range(ATTN_WIDTH // LANES):
        cols = slice(c * LANES, (c + 1) * LANES)
        q_ref[0, :, cols] = (_rope(qs[:, cols], c64, lo64, hi64, HEAD_DIM // 2) * Q_SCALE).astype(BF16)
        kr = _rope(ks[:, cols], c64, lo64, hi64, HEAD_DIM // 2)
        kb_ref[0, :, cols] = kr.astype(BF16)
        if kv_t:
            k_ref[0, 0, cols, :] = _transpose(kr)
        else:
            k_ref[0, 0, :, cols] = kr
    v = seg(SEG_V, ATTN_WIDTH)
    v_ref[0, 0] = _transpose(v) if kv_t else v
    qis = seg(SEG_QI, IDX_WIDTH)
    for c in range(IDX_WIDTH // LANES):
        cols = slice(c * LANES, (c + 1) * LANES)
        qi_ref[0, :, cols] = _rope(qis[:, cols], c32, lo32, hi32, IDX_DIM // 2).astype(BF16)
    slab = _rope(seg(SEG_MISC, LANES), cm, lom, him, IDX_DIM // 2)
    slab_ref[0] = slab
    kit_ref[0, 0] = _transpose(slab)[0:IDX_DIM, :]
    sza_ref[0] = _silu(seg(SEG_ZA, ATTN_WIDTH))
    sga_ref[0] = jax.nn.sigmoid(seg(SEG_GA, D_MODEL))


def _proj(x, left16, tabs, g, w_in, w_mix, pscale, w_po, prev, *, layer, depth, past, tm, kv_t):
    B, T, _ = x.shape
    nt = T // tm
    const = dict(pipeline_mode=pl.Buffered(1))
    row = lambda n: pl.BlockSpec((1, tm, n), lambda b, t: (b, t, 0))
    if kv_t:
        kv_shape = (depth, B, ATTN_WIDTH, T)
        kv_spec = pl.BlockSpec((1, 1, ATTN_WIDTH, tm), lambda b, t: (layer, b, 0, t))
    else:
        kv_shape = (depth, B, T, ATTN_WIDTH)
        kv_spec = pl.BlockSpec((1, 1, tm, ATTN_WIDTH), lambda b, t: (layer, b, t, 0))
    out_shape = [
        jax.ShapeDtypeStruct((B, T, D_MODEL), F32),
        jax.ShapeDtypeStruct((B, T, ATTN_WIDTH), BF16),
        jax.ShapeDtypeStruct((B, T, ATTN_WIDTH), BF16),
        jax.ShapeDtypeStruct(kv_shape, F32),
        jax.ShapeDtypeStruct(kv_shape, F32),
        jax.ShapeDtypeStruct((B, T, IDX_WIDTH), BF16),
        jax.ShapeDtypeStruct((B, T, LANES), F32),
        jax.ShapeDtypeStruct((depth, B, IDX_DIM, T), F32),
        jax.ShapeDtypeStruct((B, T, ATTN_WIDTH), F32),
        jax.ShapeDtypeStruct((B, T, D_MODEL), F32),
        jax.ShapeDtypeStruct((depth, B, HALO, POOL_WIDTH), F32),
    ]
    out_specs = [row(D_MODEL), row(ATTN_WIDTH), row(ATTN_WIDTH), kv_spec, kv_spec, row(IDX_WIDTH), row(LANES),
                 pl.BlockSpec((1, 1, IDX_DIM, tm), lambda b, t: (layer, b, 0, t)),
                 row(ATTN_WIDTH), row(D_MODEL),
                 pl.BlockSpec((1, 1, HALO, POOL_WIDTH), lambda b, t: (layer, b, 0, 0))]
    in_specs = [
        row(D_MODEL),
        pl.BlockSpec((1, HALO, POOL_WIDTH), lambda b, t: (b, 0, 0)),
        pl.BlockSpec((9, tm, LANES), lambda b, t: (0, t, 0)),
        pl.BlockSpec((1, D_MODEL), lambda b, t: (0, 0), **const),
        pl.BlockSpec((D_MODEL, PROJ_COLS), lambda b, t: (0, 0), **const),
        pl.BlockSpec((4, POOL_GROUP_WIDTH, POOL_GROUP_WIDTH), lambda b, t: (0, 0, 0), **const),
        pl.BlockSpec((1, POOL_WIDTH), lambda b, t: (0, 0), **const),
        pl.BlockSpec((POOL_WIDTH, D_MODEL), lambda b, t: (0, 0), **const),
    ]
    args = [x, left16, tabs, g, w_in, w_mix, pscale, w_po]
    aliases = {}
    if prev is not None:
        for arr, out_idx in zip(prev, (3, 4, 7, 10)):
            aliases[len(args)] = out_idx
            in_specs.append(pl.BlockSpec(memory_space=pl.ANY))
            args.append(arr)
    return pl.pallas_call(
        functools.partial(_proj_kernel, tm=tm, past=past, kv_t=kv_t, n_alias=len(aliases)),
        grid=(B, nt),
        in_specs=in_specs,
        out_specs=out_specs,
        out_shape=out_shape,
        input_output_aliases=aliases,
        scratch_shapes=[pltpu.VMEM((tm, D_MODEL), BF16),
                        pltpu.VMEM((tm + HALO, POOL_WIDTH), F32),
                        pltpu.VMEM((tm, POOL_WIDTH), BF16)],
        compiler_params=pltpu.CompilerParams(dimension_semantics=("arbitrary", "arbitrary"),
                                             vmem_limit_bytes=VMEM_LIMIT),
        name="proj",
    )(*args)


def _attend_kernel(*refs, qb, ql, kc, n_past, past, total, ksel, has_past, kv_t, final):
    if has_past:
        (q_ref, qi_ref, slab_ref, k_ref, v_ref, pk_ref, pv_ref, pki_ref) = refs[:8]
        rest = refs[8:]
    else:
        (q_ref, qi_ref, slab_ref, k_ref, v_ref) = refs[:5]
        rest = refs[5:]
    (rep_ref, rept_ref, tri_ref, sza_ref, sga_ref, gp_ref, x_ref, wao_ref, wo_ref, fg_ref) = rest[:10]
    rest = rest[10:]
    (y_ref, kbuf, vtbuf, kibuf, key_scr, rk_scr, bias_scr, ot_scr, on_scr, qm_scr, qim_scr, m_scr, a_scr, s_scr,
     p_scr) = rest
    s = pl.program_id(1)
    contract_last = (((1,), (1,)), ((), ()))

    def store_values(j, vt):
        for h in range(N_HEADS):
            vtbuf[j, h, 0:HEAD_DIM, :] = vt[h * HEAD_DIM:(h + 1) * HEAD_DIM, :].astype(BF16)
            vtbuf[j, h, HEAD_DIM:VT_ROWS, :] = jnp.ones((PACK, kc), BF16)

    if has_past:
        @pl.when(s < n_past)
        def _():
            kbuf[s] = pk_ref[0, 0].T.astype(BF16)
            store_values(s, pv_ref[0, 0])
            ki_t = jnp.dot(rept_ref[...], pki_ref[0, 0].astype(BF16), preferred_element_type=F32)
            kibuf[s] = ki_t.T.astype(BF16)

    @pl.when(s >= n_past)
    def _():
        i = s - n_past
        c = n_past + i
        slab = _pad_rows(slab_ref[0], kc)
        kbuf[c] = _pad_rows(k_ref[0], kc)
        store_values(c, v_ref[0, 0] if kv_t else _pad_rows(v_ref[0, 0], kc).T)
        kibuf[c] = jnp.dot(slab.astype(BF16), rep_ref[...], preferred_element_type=F32).astype(BF16)

        w_t = _pad_rows(slab_ref[0], ql).T[IDX_DIM:IDX_DIM + IDX_HEADS, :] * (IDX_DIM ** -0.5 * IDX_HEADS ** -0.5)
        qi = _pad_rows(qi_ref[0], ql)
        q = _pad_rows(q_ref[0], ql)
        lane = lax.broadcasted_iota(I32, (ql, LANES), 1)
        heads_per_tile = LANES // IDX_DIM
        zero = jnp.zeros((), BF16)
        for h in range(IDX_HEADS):
            tile = h // heads_per_tile
            qim_scr[h] = jnp.where(lane // IDX_DIM == h % heads_per_tile, qi[:, tile * LANES:(tile + 1) * LANES], zero)
        for h in range(N_HEADS):
            pair = h // 2
            qm_scr[h] = jnp.where(lane // HEAD_DIM == h % 2, q[:, pair * LANES:(pair + 1) * LANES], zero)

        def score_products(j, slot):
            ki = kibuf[j]
            for h in range(IDX_HEADS):
                s_scr[slot, h] = lax.dot_general(ki, qim_scr[h], contract_last, preferred_element_type=F32)

        def score_keys(slot):
            acc = w_t[0:1, :] * jnp.maximum(s_scr[slot, 0], 0.0)
            for h in range(1, IDX_HEADS):
                acc = acc + w_t[h:h + 1, :] * jnp.maximum(s_scr[slot, h], 0.0)
            bits = pltpu.bitcast(acc, I32)
            key = bits ^ ((bits >> 31) & 0x7FFFFFFF)
            return jnp.where(acc == 0.0, 0, key)

        def store_keys(j, key):
            key_scr[j] = key
            score_bits = key ^ ((key >> 31) & 0x7FFFFFFF)
            rk_scr[j] = pltpu.bitcast(score_bits & -65536, F32).astype(BF16)

        def score_step(j, slot):
            score_products(j + 1, 1 - slot)
            store_keys(j, score_keys(slot))

        def score_pair(t, carry):
            score_step(2 * t, 0)
            score_step(2 * t + 1, 1)
            return carry

        def score_own_chunk(slot):
            kpos = c * kc + lax.broadcasted_iota(I32, (kc, ql), 0)
            qpos = past + i * qb + lax.broadcasted_iota(I32, (kc, ql), 1)
            adm = (kpos // CHUNK <= qpos // CHUNK) & (kpos < total)
            store_keys(c, jnp.where(adm, score_keys(slot), INT_MIN))

        score_products(0, 0)
        lax.fori_loop(0, c // 2, score_pair, 0)

        @pl.when(c % 2 == 1)
        def _():
            score_step(c - 1, 0)
            score_own_chunk(1)

        @pl.when(c % 2 == 0)
        def _():
            score_own_chunk(0)

        one, nil = jnp.ones((), BF16), jnp.zeros((), BF16)

        def count_packed(cand, strict):
            cb = jnp.broadcast_to(cand, (PACK, ql)).astype(BF16)[None]

            def body(j, acc):
                x = rk_scr[j].reshape(kc // PACK, PACK, ql)
                hit = (x > cb) if strict else (x >= cb)
                ones = jnp.where(hit, one, nil)
                parts = [ones[r] for r in range(kc // PACK)]
                while len(parts) > 1:
                    parts = [a + b for a, b in zip(parts[::2], parts[1::2])]
                return acc + parts[0].astype(F32)
            acc = lax.fori_loop(0, c + 1, body, jnp.zeros((PACK, ql), F32))
            return acc.sum(axis=0, keepdims=True)

        def top16_as_float(k16):
            pat = k16 ^ ((k16 >> 31) & 0x7FFF)
            pat = jnp.where((pat > 0) & (pat < MIN_NORMAL_TOP16), MIN_NORMAL_TOP16, pat)
            return pltpu.bitcast(pat << 16, F32)

        def digit_search(nbits, to_float, want):
            def step(p, pre):
                cand = pre | (jnp.int32(1) << (nbits - 1 - p))
                return jnp.where(count_packed(to_float(cand), False) >= want, cand, pre)
            return lax.fori_loop(0, nbits, step, jnp.zeros((1, ql), I32))

        def load_digit(shift, upper):
            def body(j, carry):
                kj = key_scr[j]
                d = jnp.where((kj >> (shift + 8)) == upper, (kj >> shift) & 0xFF, -1)
                rk_scr[j] = d.astype(F32).astype(BF16)
                return carry
            lax.fori_loop(0, c + 1, body, 0)

        half = 1 << 15
        hi = digit_search(16, lambda u: top16_as_float(u - half), ksel) - half
        want = ksel - count_packed(top16_as_float(hi), True)
        load_digit(8, hi)
        mid = digit_search(8, lambda d: d.astype(F32), want)
        want = want - count_packed(mid.astype(F32), True)
        upper = (hi << 8) | mid
        load_digit(0, upper)
        low = digit_search(8, lambda d: d.astype(F32), want)
        thr = (upper << 8) | low
        above = count_packed(low.astype(F32), True)
        ties = count_packed(low.astype(F32), False) - above
        need = want - above
        live = thr != INT_MIN
        real = lax.broadcasted_iota(I32, (1, ql), 1) < qb
        surplus = jnp.max(jnp.where(live & real & (ties > need), 1.0, 0.0)) > 0.0

        @pl.when(jnp.logical_not(surplus))
        def _():
            def select_chunk(j, carry):
                kj = key_scr[j]
                bias_scr[j] = jnp.where((kj > thr) | ((kj == thr) & live), 0.0, NEG_BIG)
                return carry
            lax.fori_loop(0, c + 1, select_chunk, 0)

        @pl.when(surplus)
        def _():
            def select_chunk(j, seen):
                kj = key_scr[j]
                eq = kj == thr
                eqf = jnp.where(eq, 1.0, 0.0)
                rank = seen + jnp.dot(tri_ref[...], eqf.astype(BF16), preferred_element_type=F32)
                sel = (kj > thr) | (eq & (rank < need) & live)
                bias_scr[j] = jnp.where(sel, 0.0, NEG_BIG)
                return seen + eqf.reshape(kc // 8, 8, ql).sum(axis=0).sum(axis=0, keepdims=True)
            lax.fori_loop(0, c + 1, select_chunk, jnp.zeros((1, ql), F32))

        def logits(j, slot):
            for h in range(N_HEADS):
                pair = h // 2
                kj = kbuf[j, :, pair * LANES:(pair + 1) * LANES]
                s_scr[slot, h] = lax.dot_general(kj, qm_scr[h], contract_last,
                                                 preferred_element_type=F32) + bias_scr[j]

        def softmax(slot):
            for h in range(N_HEADS):
                mx = s_scr[slot, h].reshape(kc // 8, 8, ql).max(axis=0).max(axis=0, keepdims=True)
                m = m_scr[h:h + 1, :]
                m_new = jnp.maximum(m, mx)
                a_scr[slot, h:h + 1, :] = jnp.exp2(m - m_new)
                m_scr[h:h + 1, :] = m_new
            for h in range(N_HEADS):
                p_scr[slot, h] = jnp.exp2(s_scr[slot, h] - m_scr[h:h + 1, :]).astype(BF16)

        def accumulate(j, slot):
            for h in range(N_HEADS):
                ot_scr[h] = a_scr[slot, h:h + 1, :] * ot_scr[h] + jnp.dot(vtbuf[j, h], p_scr[slot, h],
                                                                          preferred_element_type=F32)

        m_scr[...] = jnp.full((N_HEADS, ql), NEG_BIG, F32)
        ot_scr[...] = jnp.zeros((N_HEADS, VT_ROWS, ql), F32)
        p_scr[1] = jnp.zeros((N_HEADS, kc, ql), BF16)
        a_scr[1] = jnp.ones((N_HEADS, ql), F32)

        def attend_step(j, slot):
            logits(j + 1, 1 - slot)
            softmax(slot)
            accumulate(jnp.maximum(j - 1, 0), 1 - slot)

        def attend_pair(t, carry):
            attend_step(2 * t, 0)
            attend_step(2 * t + 1, 1)
            return carry

        def attend_drain(last):
            softmax(last)
            accumulate(jnp.maximum(c - 1, 0), 1 - last)
            accumulate(c, last)

        logits(0, 0)
        lax.fori_loop(0, c // 2, attend_pair, 0)

        @pl.when(c % 2 == 1)
        def _():
            attend_step(c - 1, 0)
            attend_drain(1)

        @pl.when(c % 2 == 0)
        def _():
            attend_drain(0)

        for h in range(N_HEADS):
            rows = slice(h * HEAD_DIM, (h + 1) * HEAD_DIM)
            on_scr[rows, :] = ot_scr[h, 0:HEAD_DIM, :] / ot_scr[h, HEAD_DIM:HEAD_DIM + 1, :]

        a = (on_scr[...].T[0:qb, :] * sza_ref[0]).astype(BF16)
        br_attn = jnp.dot(a, wao_ref[...], preferred_element_type=F32)
        merged = gp_ref[0] + sga_ref[0] * br_attn
        y = x_ref[0] + jnp.dot(merged.astype(BF16), wo_ref[...], preferred_element_type=F32)
        if final:
            r = lax.rsqrt(jnp.mean(y * y, axis=-1, keepdims=True) + EPS)
            y = y * r * fg_ref[...]
        y_ref[0] = y


def _attend(q, qi, slab, kb, v_all, past_t, sza, sga, gp, x, w_ao, w_o, fg, *, layer, qb, kc, kv_t, final):
    B, T, _ = q.shape
    has_past = past_t is not None
    past = past_t[0].shape[3] if has_past else 0
    total = past + T
    assert past % kc == 0 and T % qb == 0 and (qb == kc or T == qb)
    n_past = past // kc
    nb = T // qb
    nc = n_past + (T + kc - 1) // kc
    ql = max(qb, LANES)
    ksel = min(TOPK_MAX, total // 4)
    lane = jnp.arange(LANES)
    rep = (lane[:, None] == (lane[None, :] % IDX_DIM)) & (lane[:, None] < IDX_DIM)
    tri = jnp.arange(kc)[:, None] > jnp.arange(kc)[None, :]

    blk = lambda s: jnp.maximum(s - n_past, 0)
    cur = lambda n: pl.BlockSpec((1, qb, n), lambda b, s: (b, blk(s), 0))
    if kv_t:
        v_spec = pl.BlockSpec((1, 1, ATTN_WIDTH, qb), lambda b, s: (layer, b, 0, blk(s)))
    else:
        v_spec = pl.BlockSpec((1, 1, qb, ATTN_WIDTH), lambda b, s: (layer, b, blk(s), 0))
    in_specs = [cur(ATTN_WIDTH), cur(IDX_WIDTH), cur(LANES), cur(ATTN_WIDTH), v_spec]
    args = [q, qi, slab, kb, v_all]
    if has_past:
        old = lambda n: pl.BlockSpec((1, 1, n, kc), lambda b, s: (layer, b, 0, jnp.minimum(s, n_past - 1)))
        in_specs += [old(ATTN_WIDTH), old(ATTN_WIDTH), old(IDX_DIM)]
        args += list(past_t)
    in_specs += [pl.BlockSpec((LANES, LANES), lambda b, s: (0, 0)),
                 pl.BlockSpec((LANES, IDX_DIM), lambda b, s: (0, 0)),
                 pl.BlockSpec((kc, kc), lambda b, s: (0, 0))]
    args += [rep.astype(BF16), rep[:IDX_DIM, :].T.astype(BF16), tri.astype(BF16)]
    const = dict(pipeline_mode=pl.Buffered(1))
    in_specs += [cur(ATTN_WIDTH), cur(D_MODEL), cur(D_MODEL), cur(D_MODEL),
                 pl.BlockSpec((ATTN_WIDTH, D_MODEL), lambda b, s: (0, 0), **const),
                 pl.BlockSpec((D_MODEL, D_MODEL), lambda b, s: (0, 0), **const),
                 pl.BlockSpec((1, D_MODEL), lambda b, s: (0, 0), **const)]
    args += [sza, sga, gp, x, w_ao, w_o, fg]
    return pl.pallas_call(
        functools.partial(_attend_kernel, qb=qb, ql=ql, kc=kc, n_past=n_past, past=past, total=total,
                          ksel=ksel, has_past=has_past, kv_t=kv_t, final=final),
        grid=(B, n_past + nb),
        in_specs=in_specs,
        out_specs=cur(D_MODEL),
        out_shape=jax.ShapeDtypeStruct((B, T, D_MODEL), F32),
        scratch_shapes=[pltpu.VMEM((nc, kc, ATTN_WIDTH), BF16),
                        pltpu.VMEM((nc, N_HEADS, VT_ROWS, kc), BF16),
                        pltpu.VMEM((nc, kc, LANES), BF16),
                        pltpu.VMEM((nc, kc, ql), I32),
                        pltpu.VMEM((nc, kc, ql), BF16),
                        pltpu.VMEM((nc, kc, ql), F32),
                        pltpu.VMEM((N_HEADS, VT_ROWS, ql), F32),
                        pltpu.VMEM((ATTN_WIDTH, ql), F32),
                        pltpu.VMEM((N_HEADS, ql, LANES), BF16),
                        pltpu.VMEM((IDX_HEADS, ql, LANES), BF16),
                        pltpu.VMEM((N_HEADS, ql), F32),
                        pltpu.VMEM((2, N_HEADS, ql), F32),
                        pltpu.VMEM((2, N_HEADS, kc, ql), F32),
                        pltpu.VMEM((2, N_HEADS, kc, ql), BF16)],
        compiler_params=pltpu.CompilerParams(dimension_semantics=("arbitrary", "arbitrary"),
                                             vmem_limit_bytes=VMEM_LIMIT),
        name="attend",
    )(*args)


def _layer(x, left16, tabs, past_t, wts, fg, prev, *, layer, depth, past, tm, qb, kc, kv_t):
    g, w_in, w_mix, pscale, w_po, w_ao, w_o = wts
    B, T, _ = x.shape
    gp, q, kb, k_all, v_all, qi, slab, kit_all, sza, sga, pstate_all = _proj(
        x, left16, tabs, g, w_in, w_mix, pscale, w_po, prev, layer=layer, depth=depth, past=past, tm=tm, kv_t=kv_t)
    y = _attend(q, qi, slab, kb, v_all, past_t, sza, sga, gp, x, w_ao, w_o, fg,
                layer=layer, qb=qb, kc=kc, kv_t=kv_t, final=layer == depth - 1)
    return y, (k_all, v_all, kit_all, pstate_all)


def kernel(x_prompt, x_sample, cache_k, cache_v, cache_kidx, state_pool, norm_g, w_in, w_pool_mix, pool_scale,
           w_pool_out, w_attn_out, w_o, final_norm_g):
    Bp, Tp, _ = x_prompt.shape
    Bs, Ts, _ = x_sample.shape
    depth = w_in.shape[0]
    past = cache_k.shape[2]
    tabs_p = _rope_tables(jnp.arange(Tp, dtype=jnp.int32))
    tabs_s = _rope_tables(past + jnp.arange(Ts, dtype=jnp.int32))
    left_p = jnp.zeros((Bp, HALO, POOL_WIDTH), F32)
    fg = final_norm_g.reshape(1, D_MODEL)
    tm_p = min(256, Tp)
    past_t = (jnp.transpose(cache_k, (0, 1, 3, 4, 2)).reshape(depth, Bs, ATTN_WIDTH, past),
              jnp.transpose(cache_v, (0, 1, 3, 4, 2)).reshape(depth, Bs, ATTN_WIDTH, past),
              jnp.transpose(cache_kidx, (0, 1, 3, 2)))

    xp, xs = x_prompt, x_sample
    prev_p = prev_s = None
    for l in range(depth):
        wts = (norm_g[l].reshape(1, D_MODEL), _prep_w_in(w_in[l]), w_pool_mix[l].astype(BF16),
               pool_scale[l].reshape(1, POOL_WIDTH), w_pool_out[l].astype(BF16), w_attn_out[l].astype(BF16),
               w_o[l].astype(BF16))
        xp, prev_p = _layer(xp, left_p, tabs_p, None, wts, fg, prev_p, layer=l, depth=depth, past=0,
                            tm=tm_p, qb=tm_p, kc=tm_p, kv_t=True)
        left_s = jnp.pad(state_pool[l], ((0, 0), (HALO - POOL_STATE, 0), (0, 0)))
        xs, prev_s = _layer(xs, left_s, tabs_s, past_t, wts, fg, prev_s, layer=l, depth=depth, past=past,
                            tm=Ts, qb=Ts, kc=256, kv_t=False)
    kp, vp, kitp, psp = prev_p
    ks, vs, kits, pss = prev_s
    heads_t = lambda a: jnp.transpose(a.reshape(depth, Bp, N_HEADS, HEAD_DIM, Tp), (0, 1, 4, 2, 3))
    heads = lambda a: a.reshape(depth, Bs, Ts, N_HEADS, HEAD_DIM)
    return (xp, xs,
            heads_t(kp), heads_t(vp), jnp.transpose(kitp, (0, 1, 3, 2)), psp[:, :, HALO - POOL_STATE:, :],
            heads(ks), heads(vs), jnp.transpose(kits, (0, 1, 3, 2)), pss[:, :, HALO - POOL_STATE:, :])
```
